```python
import math
import jax, jax.numpy as jnp
from jax import lax
import numpy as np

D_MODEL = 1024
BATCH = 4
SEQ = 4096
DEPTH = 1
DEC_BATCH = 8
DEC_SEQ = 2048
PAST_LEN = 128

POOL_WIDTH = 512
POOL_GROUPS = 4
POOL_GROUP_DIM = POOL_WIDTH // POOL_GROUPS
POOL_WINDOWS = (2, 4, 8, 16)
HYENA_WIDTH = 1024
HYENA_ORDER = 2
HYENA_IN = (HYENA_ORDER + 1) * HYENA_WIDTH
N_DIRS = 2
N_FILTER_SETS = N_DIRS * HYENA_ORDER
POS_BANDS = 16
POS_EMB = 1 + 2 * POS_BANDS
FILTER_HIDDEN = 64
DECAY_TARGET = 1e-2
FAST_DECAY_PCT = 0.3
SLOW_DECAY_PCT = 1.5
N_BRANCH = 2
IN_WIDTH = POOL_WIDTH + HYENA_IN + N_BRANCH * D_MODEL
D_FF = 4 * D_MODEL
DN_ALPHA = (2.0 * DEPTH) ** 0.25
DN_BETA = (8.0 * DEPTH) ** -0.25
LN_EPS = 1e-5
L1_EPS = 1e-6

kernel_name = 'hybrid_pool_hyena_deepnorm_encoder'


def layer_norm(x, g, b):
    xf = x.astype(jnp.float32)
    mu = jnp.mean(xf, axis=-1, keepdims=True)
    xc = xf - mu
    var = jnp.mean(xc * xc, axis=-1, keepdims=True)
    y = xc * lax.rsqrt(var + LN_EPS) * g.astype(jnp.float32) + b.astype(jnp.float32)
    return y.astype(x.dtype)


def centred_mean_minus_self(a):
    B, L, C = a.shape
    af = a.astype(jnp.float32)
    cs = jnp.concatenate([jnp.zeros((B, 1, C), jnp.float32), jnp.cumsum(af, axis=1)], axis=1)
    pos = jnp.arange(L)
    means = []
    for g, w in enumerate(POOL_WINDOWS):
        lo = jnp.clip(pos - w // 2, 0, L)
        hi = jnp.clip(pos + (w - w // 2), 0, L)
        csg = cs[:, :, g * POOL_GROUP_DIM:(g + 1) * POOL_GROUP_DIM]
        cnt = (hi - lo).astype(jnp.float32)[None, :, None]
        means.append((csg[:, hi] - csg[:, lo]) / cnt)
    return (jnp.concatenate(means, axis=-1) - af).astype(a.dtype)


def pool_branch(a, w_pool, b_pool, pool_scale, w_pool_proj):
    B, L, _ = a.shape
    p = centred_mean_minus_self(a).reshape(B, L, POOL_GROUPS, POOL_GROUP_DIM)
    p = jnp.einsum('blgc,gcd->blgd', p, w_pool) + b_pool
    p = p.reshape(B, L, POOL_WIDTH) * pool_scale
    return p @ w_pool_proj


def positional_features(L):
    t = jnp.linspace(0.0, 1.0, L, dtype=jnp.float32)[:, None]
    w = (2.0 * math.pi / L) * jnp.arange(L, dtype=jnp.float32)[:, None]
    f = jnp.linspace(1e-4, POS_BANDS - 1, POS_BANDS, dtype=jnp.float32)[None, :]
    z = jnp.concatenate([t, jnp.cos(f * w), -jnp.sin(f * w)], axis=-1)
    return z, t


def hyena_filter_spectra(L, w_f1, b_f1, freq_f1, w_f2, b_f2, freq_f2, w_f_out, decay_rate):
    f32 = jnp.float32
    z, t = positional_features(L)
    h = jnp.sin(freq_f1.astype(f32) * (z @ w_f1.astype(f32) + b_f1.astype(f32)))
    h = jnp.sin(freq_f2.astype(f32) * (h @ w_f2.astype(f32) + b_f2.astype(f32)))
    h = (h @ w_f_out.astype(f32)).reshape(L, N_FILTER_SETS, HYENA_WIDTH)
    h = h * jnp.exp(-t[:, :, None] * jnp.abs(decay_rate.astype(f32))[None])
    h = h.reshape(L, HYENA_ORDER, N_DIRS, HYENA_WIDTH)
    lag_pos = (jnp.arange(L) > 0).astype(f32)
    dir_mask = jnp.stack([jnp.ones((L,), f32), lag_pos], axis=1)[:, None, :, None]
    h = h * dir_mask
    h = h / (jnp.sum(jnp.abs(h), axis=(0, 2), keepdims=True) + L1_EPS)
    k = jnp.concatenate([h[:, :, 0], jnp.zeros((1, HYENA_ORDER, HYENA_WIDTH), f32), h[:0:-1, :, 1]], axis=0)
    return jnp.fft.rfft(k, axis=0)


def fft_long_conv(u, kf, bias):
    L = u.shape[1]
    uf = u.astype(jnp.float32)
    U = jnp.fft.rfft(uf, n=2 * L, axis=1)
    y = jnp.fft.irfft(U * kf[None], n=2 * L, axis=1)[:, :L]
    return (y + uf * bias.astype(jnp.float32)).astype(u.dtype)


def short_conv_centred(u, w, b):
    up = jnp.pad(u, ((0, 0), (1, 1), (0, 0)))
    return up[:, :-2] * w[0] + up[:, 1:-1] * w[1] + up[:, 2:] * w[2] + b


def hyena_branch(u3, conv_w, conv_b, kf, hyena_bias, w_hyena_proj):
    uc = short_conv_centred(u3, conv_w, conv_b)
    x1 = uc[..., :HYENA_WIDTH]
    x2 = uc[..., HYENA_WIDTH:2 * HYENA_WIDTH]
    z = uc[..., 2 * HYENA_WIDTH:]
    for o, gate in enumerate((x1, x2)):
        z = gate * fft_long_conv(z, kf[:, o], hyena_bias[o])
    return z @ w_hyena_proj


def token_mixer(h, p, l):
    B, L, _ = h.shape
    proj = h @ p['w_in'][l] + p['b_in'][l]
    a = proj[..., :POOL_WIDTH]
    u3 = proj[..., POOL_WIDTH:POOL_WIDTH + HYENA_IN]
    gl = proj[..., POOL_WIDTH + HYENA_IN:]
    ya = pool_branch(a, p['w_pool'][l], p['b_pool'][l], p['pool_scale'][l], p['w_pool_proj'][l])
    kf = hyena_filter_spectra(L, p['w_f1'][l], p['b_f1'][l], p['freq_f1'][l], p['w_f2'][l], p['b_f2'][l],
                              p['freq_f2'][l], p['w_f_out'][l], p['decay_rate'][l])
    yb = hyena_branch(u3, p['conv_w'][l], p['conv_b'][l], kf, p['hyena_bias'][l], p['w_hyena_proj'][l])
    g = jax.nn.sigmoid(gl.astype(jnp.float32)).astype(h.dtype).reshape(B, L, N_BRANCH, D_MODEL)
    m = g[:, :, 0] * ya + g[:, :, 1] * yb
    return m @ p['w_o'][l] + p['b_o'][l]


def channel_mixer(h, p, l):
    u = jax.nn.relu(h @ p['w_ff1'][l] + p['b_ff1'][l])
    return (u * u) @ p['w_ff2'][l] + p['b_ff2'][l]


def encoder_trunk(x, p):
    h = layer_norm(x, p['ln_in_g'], p['ln_in_b'])
    for l in range(DEPTH):
        h = layer_norm(DN_ALPHA * h + token_mixer(h, p, l), p['ln1_g'][l], p['ln1_b'][l])
        h = layer_norm(DN_ALPHA * h + channel_mixer(h, p, l), p['ln2_g'][l], p['ln2_b'][l])
    return h


def setup_inputs(seed: int = 0) -> dict:
    key = jax.random.key(seed)
    ks = jax.random.split(key, 40)
    f32 = jnp.float32

    def nrm(k, shape, scale):
        return jax.random.normal(k, shape, f32) * scale

    Dp = DEPTH
    min_decay = math.log(DECAY_TARGET) / SLOW_DECAY_PCT
    max_decay = math.log(DECAY_TARGET) / FAST_DECAY_PCT
    decay_base = jnp.linspace(min_decay, max_decay, HYENA_WIDTH, dtype=f32)
    dshape = (Dp, N_FILTER_SETS, HYENA_WIDTH)
    return {
        'x_prompt': nrm(ks[0], (BATCH, SEQ, D_MODEL), 1.0),
        'x_sample': nrm(ks[1], (DEC_BATCH, DEC_SEQ, D_MODEL), 1.0),
        'ln_in_g': 1.0 + nrm(ks[2], (D_MODEL,), 0.01),
        'ln_in_b': nrm(ks[3], (D_MODEL,), 0.01),
        'w_in': nrm(ks[4], (Dp, D_MODEL, IN_WIDTH), D_MODEL ** -0.5),
        'b_in': nrm(ks[5], (Dp, IN_WIDTH), 0.01),
        'w_pool': nrm(ks[6], (Dp, POOL_GROUPS, POOL_GROUP_DIM, POOL_GROUP_DIM), POOL_GROUP_DIM ** -0.5),
        'b_pool': nrm(ks[7], (Dp, POOL_GROUPS, POOL_GROUP_DIM), 0.01),
        'pool_scale': 1.0 + nrm(ks[8], (Dp, POOL_WIDTH), 0.1),
        'w_pool_proj': nrm(ks[9], (Dp, POOL_WIDTH, D_MODEL), POOL_WIDTH ** -0.5),
        'conv_w': nrm(ks[10], (Dp, 3, HYENA_IN), 3.0 ** -0.5),
        'conv_b': nrm(ks[11], (Dp, HYENA_IN), 0.01),
        'w_f1': nrm(ks[12], (Dp, POS_EMB, FILTER_HIDDEN), POS_EMB ** -0.5),
        'b_f1': nrm(ks[13], (Dp, FILTER_HIDDEN), 0.1),
        'freq_f1': 1.0 + nrm(ks[14], (Dp, FILTER_HIDDEN), 0.1),
        'w_f2': nrm(ks[15], (Dp, FILTER_HIDDEN, FILTER_HIDDEN), FILTER_HIDDEN ** -0.5),
        'b_f2': nrm(ks[16], (Dp, FILTER_HIDDEN), 0.1),
        'freq_f2': 1.0 + nrm(ks[17], (Dp, FILTER_HIDDEN), 0.1),
        'w_f_out': nrm(ks[18], (Dp, FILTER_HIDDEN, N_FILTER_SETS * HYENA_WIDTH), FILTER_HIDDEN ** -0.5),
        'decay_rate': jnp.broadcast_to(decay_base, dshape) + nrm(ks[19], dshape, 0.1),
        'hyena_bias': nrm(ks[20], (Dp, HYENA_ORDER, HYENA_WIDTH), 0.1),
        'w_hyena_proj': nrm(ks[21], (Dp, HYENA_WIDTH, D_MODEL), HYENA_WIDTH ** -0.5),
        'w_o': nrm(ks[22], (Dp, D_MODEL, D_MODEL), D_MODEL ** -0.5 * DN_BETA),
        'b_o': nrm(ks[23], (Dp, D_MODEL), 0.01),
        'ln1_g': 1.0 + nrm(ks[24], (Dp, D_MODEL), 0.01),
        'ln1_b': nrm(ks[25], (Dp, D_MODEL), 0.01),
        'w_ff1': nrm(ks[26], (Dp, D_MODEL, D_FF), D_MODEL ** -0.5),
        'b_ff1': nrm(ks[27], (Dp, D_FF), 0.01),
        'w_ff2': nrm(ks[28], (Dp, D_FF, D_MODEL), D_FF ** -0.5 * DN_BETA),
        'b_ff2': nrm(ks[29], (Dp, D_MODEL), 0.01),
        'ln2_g': 1.0 + nrm(ks[30], (Dp, D_MODEL), 0.01),
        'ln2_b': nrm(ks[31], (Dp, D_MODEL), 0.01),
    }


def reference(x_prompt, x_sample, ln_in_g, ln_in_b, w_in, b_in, w_pool, b_pool, pool_scale, w_pool_proj,
              conv_w, conv_b, w_f1, b_f1, freq_f1, w_f2, b_f2, freq_f2, w_f_out, decay_rate, hyena_bias,
              w_hyena_proj, w_o, b_o, ln1_g, ln1_b, w_ff1, b_ff1, w_ff2, b_ff2, ln2_g, ln2_b):
    params = {
        'ln_in_g': ln_in_g, 'ln_in_b': ln_in_b, 'w_in': w_in, 'b_in': b_in,
        'w_pool': w_pool, 'b_pool': b_pool, 'pool_scale': pool_scale, 'w_pool_proj': w_pool_proj,
        'conv_w': conv_w, 'conv_b': conv_b, 'w_f1': w_f1, 'b_f1': b_f1, 'freq_f1': freq_f1,
        'w_f2': w_f2, 'b_f2': b_f2, 'freq_f2': freq_f2, 'w_f_out': w_f_out, 'decay_rate': decay_rate,
        'hyena_bias': hyena_bias, 'w_hyena_proj': w_hyena_proj, 'w_o': w_o, 'b_o': b_o,
        'ln1_g': ln1_g, 'ln1_b': ln1_b, 'w_ff1': w_ff1, 'b_ff1': b_ff1, 'w_ff2': w_ff2, 'b_ff2': b_ff2,
        'ln2_g': ln2_g, 'ln2_b': ln2_b,
    }
    y_prompt = encoder_trunk(x_prompt, params)
    y_sample = encoder_trunk(x_sample, params)
    return (y_prompt, y_sample)
```

```python
import functools
import math

import numpy as np
import jax
import jax.numpy as jnp
from jax import lax
from jax.experimental import pallas as pl
from jax.experimental.pallas import tpu as pltpu

F32 = jnp.float32
BF16 = jnp.bfloat16

V7X_LANES = 128
V7X_SUBLANES = 8
V7X_VMEM_BYTES = 64 * 1024 * 1024
VMEM_LIMIT_BYTES = V7X_VMEM_BYTES - 8 * 1024 * 1024

POOL_WINDOWS = (2, 4, 8, 16)
POOL_HALO = max(max(w // 2, w - 1 - w // 2) for w in POOL_WINDOWS)
N_DIRS = 2
HYENA_ORDER = 2
POS_BANDS = 16
LN_EPS = 1e-5
L1_EPS = 1e-6
DEPTH = 1
DN_ALPHA = (2.0 * DEPTH) ** 0.25

ROW_PAD = V7X_SUBLANES
TOKEN_TILE = 512


def _compiler_params(semantics):
    return pltpu.CompilerParams(dimension_semantics=semantics, vmem_limit_bytes=VMEM_LIMIT_BYTES)


def _const_spec(shape):
    nd = len(shape)
    return pl.BlockSpec(shape, lambda *_: (0,) * nd, pipeline_mode=pl.Buffered(1))


def _split_factors(seq_len):
    n2 = 64
    assert seq_len % n2 == 0
    return seq_len // n2, n2


def _dft_tables(n1, n2):
    seq = n1 * n2
    i1 = np.arange(n1)
    th = np.pi * np.outer(i1, i1) / n1
    f1 = np.zeros((2 * n1, n1))
    f1[:n1] = np.cos(th)
    f1[n1] = (-1.0) ** i1
    f1[n1 + 1:] = -np.sin(th[1:])
    g = np.zeros((n1, 2 * n1))
    g[:, :n1] = np.cos(th.T)
    g[:, n1] = (-1.0) ** i1
    g[:, n1 + 1:] = -np.sin(th.T[:, 1:])
    i2 = np.arange(n2)
    base = 2 * np.pi * np.outer(i2, i2) / n2
    m = np.zeros((n1, 2 * n2, 2 * n2))
    mi = np.zeros((n1, 2 * n2, 2 * n2))
    for k in range(1, n1):
        phi = base + 2 * np.pi * k * i2[None, :] / (2 * seq)
        c, s = np.cos(phi), np.sin(phi)
        m[k] = np.block([[c, s], [-s, c]])
        mi[k] = np.block([[c.T, -s.T], [s.T, c.T]]) / seq
    phin = base + np.pi * i2[None, :] / n2
    z = np.zeros((n2, n2))
    ms = np.block([[np.cos(base), z], [-np.sin(base), z], [z, np.cos(phin)], [z, -np.sin(phin)]])
    mis = np.block([[np.cos(base).T, -np.sin(base).T, z, z],
                    [z, z, np.cos(phin).T, -np.sin(phin).T]]) / (2 * seq)
    return dict(f1=f1, g=g, m=m, mi=mi, ms=ms, mis=mis)


def _hi_lo(x):
    hi = jnp.asarray(x, F32).astype(BF16)
    lo = (jnp.asarray(x, F32) - hi.astype(F32)).astype(BF16)
    return hi, lo


def _perm_positions(n1, n2):
    return (np.arange(n2)[:, None] + n2 * np.arange(n1)[None, :]).reshape(-1)


def _positional_features(seq_len, n1, n2):
    pos = _perm_positions(n1, n2).astype(np.float64)
    t = pos / (seq_len - 1)
    w = 2.0 * math.pi * pos / seq_len
    f = np.linspace(1e-4, POS_BANDS - 1, POS_BANDS)
    z = np.concatenate([t[:, None], np.cos(f[None] * w[:, None]), -np.sin(f[None] * w[:, None])], axis=-1)
    zp = np.zeros((seq_len, 40))
    zp[:, :z.shape[1]] = z
    return zp, t[:, None]


def _layer_norm(x, g, b):
    mu = jnp.mean(x, axis=-1, keepdims=True)
    xc = x - mu
    var = jnp.mean(xc * xc, axis=-1, keepdims=True)
    return xc * lax.rsqrt(var + LN_EPS) * g + b


def _dot(a, b):
    return jnp.dot(a, b, preferred_element_type=F32)


def _dot3(a_hi, a_lo, b):
    b_hi = b.astype(BF16)
    b_lo = (b - b_hi.astype(F32)).astype(BF16)
    return _dot(a_hi, b_hi) + (_dot(a_hi, b_lo) + _dot(a_lo, b_hi))


def _store_slabs(dst_ref, row0, val):
    rows = val.shape[0]
    for j in range(dst_ref.shape[0]):
        dst_ref[j, pl.ds(row0, rows), :] = val[:, j * V7X_LANES:(j + 1) * V7X_LANES]


def _gather_pair(src_ref, start_a, start_b, count, pitch):
    cols = []
    for j in range(src_ref.shape[0]):
        a = src_ref[j, pl.ds(start_a, count, stride=pitch), :]
        b = src_ref[j, pl.ds(start_b, count, stride=pitch), :]
        cols.append(jnp.concatenate([a, b], axis=0))
    return cols[0] if len(cols) == 1 else jnp.concatenate(cols, axis=1)


def _complex_mul(x, k, half):
    xr, xi = x[:half], x[half:]
    kr, ki = k[:half], k[half:]
    return jnp.concatenate([xr * kr - xi * ki, xr * ki + xi * kr], axis=0)


def _inproj_kernel(x_ref, g_ref, b_ref, w_ref, bias_ref, a_ref, u_ref, gate_ref, h_scr, *, n1, d, chunk):
    j_per = x_ref.shape[1] // d
    for j in range(j_per):
        xj = x_ref[:, j * d:(j + 1) * d]
        h_scr[j * n1:(j + 1) * n1, :] = _layer_norm(xj, g_ref[...], b_ref[...]).astype(BF16)
    h = h_scr[...]
    wa, wu = a_ref.shape[1], u_ref.shape[1]
    total = w_ref.shape[1]
    for c in range(0, total, chunk):
        y = _dot(h, w_ref[:, c:c + chunk]) + bias_ref[:, c:c + chunk]
        if c < wa:
            a_ref[:, c:c + chunk] = y
        elif c < wa + wu:
            u_ref[:, c - wa:c - wa + chunk] = y
        else:
            gate_ref[:, c - wa - wu:c - wa - wu + chunk] = jax.nn.sigmoid(y)


def _inproj(xv, ln_g, ln_b, w_in, b_in, *, n1, n2, widths):
    bsz = xv.shape[0]
    d = w_in.shape[0]
    seq = n1 * n2
    j_per = TOKEN_TILE // n1
    wa, wu, wg = widths
    chunk = 512
    assert wa % chunk == 0 and wu % chunk == 0 and wg % chunk == 0 and n2 % j_per == 0
    out_shapes = tuple(jax.ShapeDtypeStruct((bsz, seq, w), F32) for w in widths)
    tok = lambda w: pl.BlockSpec((None, TOKEN_TILE, w), lambda b, t: (b, t, 0))
    return pl.pallas_call(
        functools.partial(_inproj_kernel, n1=n1, d=d, chunk=chunk),
        grid=(bsz, n2 // j_per),
        in_specs=[
            pl.BlockSpec((None, n1, j_per * d), lambda b, t: (b, 0, t)),
            _const_spec((1, d)), _const_spec((1, d)),
            _const_spec(w_in.shape), _const_spec(b_in.shape),
        ],
        out_specs=[tok(wa), tok(wu), tok(wg)],
        out_shape=out_shapes,
        scratch_shapes=[pltpu.VMEM((TOKEN_TILE, d), BF16)],
        compiler_params=_compiler_params(("parallel", "parallel")),
        name="inproj",
    )(xv, ln_g, ln_b, w_in, b_in)


def _pool_kernel(a_ref, o_ref, e_ref, *, n1, n2):
    seq = n1 * n2
    halo = POOL_HALO
    rows = lax.broadcasted_iota(jnp.int32, (n1, V7X_LANES), 0)

    def body(win):
        lo = win // 2
        hi = win - 1 - lo

        def copy(i, c):
            r = pl.multiple_of(i * n1, n1)
            e_ref[pl.ds(halo * n1 + r, n1), :] = a_ref[pl.ds(r, n1), :]
            return c

        lax.fori_loop(0, n2, copy, 0)
        for m in range(1, lo + 1):
            blk = a_ref[pl.ds((n2 - m) * n1 - 1, n1), :]
            e_ref[pl.ds((halo - m) * n1, n1), :] = jnp.where(rows == 0, 0.0, blk)
        for m in range(hi):
            blk = a_ref[pl.ds(m * n1 + 1, n1), :]
            e_ref[pl.ds((halo + n2 + m) * n1, n1), :] = jnp.where(rows == n1 - 1, 0.0, blk)

        def step(i, c):
            r = pl.multiple_of(i * n1, n1)
            acc = e_ref[pl.ds((halo - lo) * n1 + r, n1), :]
            for dlt in range(-lo + 1, hi + 1):
                acc = acc + e_ref[pl.ds((halo + dlt) * n1 + r, n1), :]
            pos = rows * n2 + i
            cnt = jnp.minimum(pos + (win - lo), seq) - jnp.maximum(pos - lo, 0)
            o_ref[pl.ds(r, n1), :] = acc / cnt.astype(F32) - a_ref[pl.ds(r, n1), :]
            return c

        lax.fori_loop(0, n2, step, 0)

    grp = pl.program_id(1)
    for gi, win in enumerate(POOL_WINDOWS):
        pl.when(grp == gi)(functools.partial(body, win))


def _pool(a, *, n1, n2):
    bsz, seq, width = a.shape
    ngrp = len(POOL_WINDOWS)
    assert width == ngrp * V7X_LANES
    spec = pl.BlockSpec((None, seq, V7X_LANES), lambda b, g: (b, 0, g))
    return pl.pallas_call(
        functools.partial(_pool_kernel, n1=n1, n2=n2),
        grid=(bsz, ngrp),
        in_specs=[spec],
        out_specs=spec,
        out_shape=jax.ShapeDtypeStruct(a.shape, F32),
        scratch_shapes=[pltpu.VMEM(((n2 + 2 * POOL_HALO) * n1, V7X_LANES), F32)],
        compiler_params=_compiler_params(("parallel", "parallel")),
        name="pool",
    )(a)


def _filter_kernel(z_ref, t_ref, w1_ref, b1_ref, q1_ref, w2_ref, b2_ref, q2_ref, wo_ref, dec_ref,
                   f1h_ref, f1l_ref, mh_ref, ml_ref, msh_ref, msl_ref,
                   kf_ref, hid_ref, h_ref, s1_ref, *, n1, n2):
    seq = n1 * n2
    pitch = 2 * n1 + ROW_PAD
    hp = lax.Precision.HIGHEST
    nset = h_ref.shape[0]
    rb = 256

    def mlp(i, c):
        r = pl.multiple_of(i * rb, rb)
        z = z_ref[pl.ds(r, rb), :]
        h = jnp.sin(q1_ref[...] * (jnp.dot(z, w1_ref[...], precision=hp, preferred_element_type=F32) + b1_ref[...]))
        h = jnp.sin(q2_ref[...] * (jnp.dot(h, w2_ref[...], precision=hp, preferred_element_type=F32) + b2_ref[...]))
        hid_ref[pl.ds(r, rb), :] = h
        return c

    lax.fori_loop(0, seq // rb, mlp, 0)

    sums = []
    for s in range(nset):
        rate = jnp.abs(dec_ref[s:s + 1, :])

        def gen(i, acc, s=s, rate=rate):
            r = pl.multiple_of(i * rb, rb)
            h = jnp.dot(hid_ref[pl.ds(r, rb), :], wo_ref[s], precision=hp, preferred_element_type=F32)
            h = h * jnp.exp(-t_ref[pl.ds(r, rb), :] * rate)
            if s % N_DIRS == 1:
                row = lax.broadcasted_iota(jnp.int32, h.shape, 0) + r
                h = jnp.where(row == 0, 0.0, h)
            h_ref[s, pl.ds(r, rb), :] = h
            return acc + jnp.sum(jnp.abs(h), axis=0, keepdims=True)

        sums.append(lax.fori_loop(0, seq // rb, gen, jnp.zeros((1, h_ref.shape[2]), F32)))

    for s in range(nset):
        o, dr = divmod(s, N_DIRS)
        denom = sums[o * N_DIRS] + sums[o * N_DIRS + 1] + L1_EPS

        def stage1(i, c, s=s, denom=denom):
            r = pl.multiple_of(i * n1, n1)
            blk = h_ref[s, pl.ds(r, n1), :] / denom
            _store_slabs(s1_ref, i * pitch, _dot3(f1h_ref[...], f1l_ref[...], blk))
            return c

        lax.fori_loop(0, n2, stage1, 0)

        def emit(row0, x, half, dr=dr, o=o):
            if dr == 0:
                kf_ref[o, pl.ds(row0, 2 * half), :] = x
            else:
                sign = jnp.where(lax.broadcasted_iota(jnp.int32, x.shape, 0) < half, 1.0, -1.0)
                kf_ref[o, pl.ds(row0, 2 * half), :] += sign * x

        b0 = _gather_pair(s1_ref, 0, n1, n2, pitch)
        x0 = _dot3(msh_ref[...], msl_ref[...], b0)
        emit(0, x0[:2 * n2], n2)
        emit(2 * n2, x0[2 * n2:], n2)

        def stage2(k, c, emit=emit):
            bk = _gather_pair(s1_ref, k, n1 + k, n2, pitch)
            emit(pl.multiple_of((k + 1) * 2 * n2, 2 * n2), _dot3(mh_ref[k], ml_ref[k], bk), n2)
            return c

        lax.fori_loop(1, n1, stage2, 0)


def _filter_spectra(p, tabs, *, n1, n2, width):
    seq = n1 * n2
    cb = V7X_LANES
    nset = N_DIRS * HYENA_ORDER
    nrow = 2 * n2 * (n1 + 1)
    pitch = 2 * n1 + ROW_PAD
    zfeat, tcol = _positional_features(seq, n1, n2)
    hid = p['w_f2'].shape[0]
    w1 = jnp.zeros((zfeat.shape[1], hid), F32).at[:p['w_f1'].shape[0]].set(p['w_f1'])
    wo = p['w_f_out'].reshape(hid, nset, width).transpose(1, 0, 2)
    row = lambda v: v.reshape(1, -1)
    operands = [jnp.asarray(zfeat, F32), jnp.asarray(tcol, F32), w1, row(p['b_f1']), row(p['freq_f1']),
                p['w_f2'], row(p['b_f2']), row(p['freq_f2']), wo, p['decay_rate'],
                *tabs['f1'], *tabs['m'], *tabs['ms']]
    in_specs = [_const_spec(o.shape) for o in operands]
    in_specs[8] = pl.BlockSpec((nset, hid, cb), lambda c: (0, 0, c))
    in_specs[9] = pl.BlockSpec((nset, cb), lambda c: (0, c))
    return pl.pallas_call(
        functools.partial(_filter_kernel, n1=n1, n2=n2),
        grid=(width // cb,),
        in_specs=in_specs,
        out_specs=pl.BlockSpec((HYENA_ORDER, nrow, cb), lambda c: (0, 0, c)),
        out_shape=jax.ShapeDtypeStruct((HYENA_ORDER, nrow, width), F32),
        scratch_shapes=[pltpu.VMEM((seq, hid), F32), pltpu.VMEM((nset, seq, cb), F32),
                        pltpu.VMEM((cb // V7X_LANES, n2 * pitch, V7X_LANES), F32)],
        compiler_params=_compiler_params(("parallel",)),
        name="filter_spectra",
    )(*operands)


def _hyena_kernel(x1_ref, x2_ref, v_ref, cw1_ref, cw2_ref, cwv_ref, cb1_ref, cb2_ref, cbv_ref,
                  kf_ref, hb_ref, f1_ref, g_ref, m_ref, mi_ref, ms_ref, mis_ref,
                  z_ref, u_scr, s1_ref, s2_ref, *, n1, n2):
    pitch1 = 2 * n1 + ROW_PAD
    pitch2 = 2 * n2 + ROW_PAD
    cbw = z_ref.shape[1]
    rows = lax.broadcasted_iota(jnp.int32, (n1, cbw), 0)

    def short_conv(ref, w_ref, b_ref, i, kind):
        r = i * n1 if kind != "mid" else pl.multiple_of(i * n1, n1)
        cur = ref[pl.ds(r, n1), :]
        if kind == "first":
            prev = jnp.where(rows == 0, 0.0, ref[pl.ds((n2 - 1) * n1 - 1, n1), :])
        else:
            prev = ref[pl.ds(r - n1, n1), :]
        if kind == "last":
            nxt = jnp.where(rows == n1 - 1, 0.0, ref[pl.ds(1, n1), :])
        else:
            nxt = ref[pl.ds(r + n1, n1), :]
        return prev * w_ref[0:1, :] + cur * w_ref[1:2, :] + nxt * w_ref[2:3, :] + b_ref[...]

    def for_blocks(fn):
        fn(0, "first")
        lax.fori_loop(1, n2 - 1, lambda i, c: (fn(i, "mid"), c)[1], 0)
        fn(n2 - 1, "last")

    def stage1(i, u):
        _store_slabs(s1_ref, i * pitch1, _dot(f1_ref[...], u.astype(BF16)))

    def load_v(i, kind):
        u = short_conv(v_ref, cwv_ref, cbv_ref, i, kind)
        r = i * n1 if kind != "mid" else pl.multiple_of(i * n1, n1)
        u_scr[pl.ds(r, n1), :] = u
        stage1(i, u)

    def middle(o):
        b0 = _gather_pair(s1_ref, 0, n1, n2, pitch1).astype(BF16)
        x0 = _dot(ms_ref[...], b0)
        y0 = jnp.concatenate([_complex_mul(x0[:2 * n2], kf_ref[o, 0:2 * n2, :], n2),
                              _complex_mul(x0[2 * n2:], kf_ref[o, 2 * n2:4 * n2, :], n2)], axis=0)
        _store_slabs(s2_ref, 0, _dot(mis_ref[...], y0.astype(BF16)))

        def slot(k, c):
            bk = _gather_pair(s1_ref, k, n1 + k, n2, pitch1).astype(BF16)
            xk = _dot(m_ref[k], bk)
            kf = kf_ref[o, pl.ds(pl.multiple_of((k + 1) * 2 * n2, 2 * n2), 2 * n2), :]
            yk = _complex_mul(xk, kf, n2).astype(BF16)
            _store_slabs(s2_ref, k * pitch2, _dot(mi_ref[k], yk))
            return c

        lax.fori_loop(1, n1, slot, 0)

    def finish(o, gate_ref, gw_ref, gb_ref, i, kind):
        dk = _gather_pair(s2_ref, i, n2 + i, n1, pitch2).astype(BF16)
        r = i * n1 if kind != "mid" else pl.multiple_of(i * n1, n1)
        u = u_scr[pl.ds(r, n1), :]
        conv = _dot(g_ref[...], dk) + u * hb_ref[o:o + 1, :]
        z = short_conv(gate_ref, gw_ref, gb_ref, i, kind) * conv
        if o + 1 < HYENA_ORDER:
            u_scr[pl.ds(r, n1), :] = z
            stage1(i, z)
        else:
            z_ref[pl.ds(r, n1), :] = z

    for_blocks(load_v)
    middle(0)
    for_blocks(functools.partial(finish, 0, x1_ref, cw1_ref, cb1_ref))
    middle(1)
    for_blocks(functools.partial(finish, 1, x2_ref, cw2_ref, cb2_ref))


def _hyena(u3, conv_w, conv_b, kf, hyena_bias, tabs, *, n1, n2, width, cb):
    bsz, seq, _ = u3.shape
    ncb = width // cb
    nrow = kf.shape[1]
    pitch1 = 2 * n1 + ROW_PAD
    pitch2 = 2 * n2 + ROW_PAD
    nslab = cb // V7X_LANES
    seq_spec = lambda part: pl.BlockSpec((None, seq, cb), lambda c, b: (b, 0, part * ncb + c))
    tap_spec = lambda part: pl.BlockSpec((conv_w.shape[0], cb), lambda c, b: (0, part * ncb + c))
    bias_spec = lambda part: pl.BlockSpec((1, cb), lambda c, b: (0, part * ncb + c))
    consts = [tabs['f1'][0], tabs['g'][0], tabs['m'][0], tabs['mi'][0], tabs['ms'][0], tabs['mis'][0]]
    return pl.pallas_call(
        functools.partial(_hyena_kernel, n1=n1, n2=n2),
        grid=(ncb, bsz),
        in_specs=[seq_spec(0), seq_spec(1), seq_spec(2),
                  tap_spec(0), tap_spec(1), tap_spec(2), bias_spec(0), bias_spec(1), bias_spec(2),
                  pl.BlockSpec((HYENA_ORDER, nrow, cb), lambda c, b: (0, 0, c), pipeline_mode=pl.Buffered(1)),
                  pl.BlockSpec((HYENA_ORDER, cb), lambda c, b: (0, c)),
                  *[_const_spec(t.shape) for t in consts]],
        out_specs=pl.BlockSpec((None, seq, cb), lambda c, b: (b, 0, c)),
        out_shape=jax.ShapeDtypeStruct((bsz, seq, width), F32),
        scratch_shapes=[pltpu.VMEM((seq, cb), F32),
                        pltpu.VMEM((nslab, n2 * pitch1, V7X_LANES), F32),
                        pltpu.VMEM((nslab, n1 * pitch2, V7X_LANES), F32)],
        compiler_params=_compiler_params(("parallel", "parallel")),
        name="hyena",
    )(u3, u3, u3, conv_w, conv_w, conv_w, conv_b, conv_b, conv_b, kf, hyena_bias, *consts)


def _merge_kernel(x_ref, pm_ref, z_ref, gate_ref, lng_ref, lnb_ref, wp_ref, bp_ref, ps_ref, wpp_ref,
                  whp_ref, wo_ref, bo_ref, g1_ref, b1_ref, h_ref, *, n1, d):
    j_per = x_ref.shape[1] // d
    ngrp = wp_ref.shape[0]
    gd = wp_ref.shape[1]
    pm = pm_ref[...].astype(BF16)
    p = jnp.concatenate([_dot(pm[:, g * gd:(g + 1) * gd], wp_ref[g]) for g in range(ngrp)], axis=1)
    p = (p + bp_ref[...]) * ps_ref[...]
    ya = _dot(p.astype(BF16), wpp_ref[...])
    yb = _dot(z_ref[...].astype(BF16), whp_ref[...])
    m = gate_ref[:, :d] * ya + gate_ref[:, d:] * yb
    mo = _dot(m.astype(BF16), wo_ref[...]) + bo_ref[...]
    for j in range(j_per):
        h0 = _layer_norm(x_ref[:, j * d:(j + 1) * d], lng_ref[...], lnb_ref[...])
        rs = slice(j * n1, (j + 1) * n1)
        h_ref[rs, :] = _layer_norm(DN_ALPHA * h0 + mo[rs], g1_ref[...], b1_ref[...])


def _merge(xv, pm, z, gate, consts, *, n1, n2):
    bsz, seq, d = z.shape
    j_per = TOKEN_TILE // n1
    tok = lambda w: pl.BlockSpec((None, TOKEN_TILE, w), lambda b, t: (b, t, 0))
    return pl.pallas_call(
        functools.partial(_merge_kernel, n1=n1, d=d),
        grid=(bsz, n2 // j_per),
        in_specs=[pl.BlockSpec((None, n1, j_per * d), lambda b, t: (b, 0, t)),
                  tok(pm.shape[2]), tok(d), tok(gate.shape[2]),
                  *[_const_spec(c.shape) for c in consts]],
        out_specs=tok(d),
        out_shape=jax.ShapeDtypeStruct((bsz, seq, d), F32),
        compiler_params=_compiler_params(("parallel", "parallel")),
        name="merge",
    )(xv, pm, z, gate, *consts)


def _ffn_kernel(h_ref, w1_ref, b1_ref, w2_ref, b2_ref, g_ref, b_ref, y_ref, *, n1, d, chunk):
    h = h_ref[...]
    hb = h.astype(BF16)
    dff = w1_ref.shape[1]
    acc = jnp.zeros(h.shape, F32)
    for c in range(0, dff, chunk):
        u = jnp.maximum(_dot(hb, w1_ref[:, c:c + chunk]) + b1_ref[:, c:c + chunk], 0.0)
        acc = acc + _dot((u * u).astype(BF16), w2_ref[c:c + chunk, :])
    y = _layer_norm(DN_ALPHA * h + acc + b2_ref[...], g_ref[...], b_ref[...])
    for j in range(y_ref.shape[1] // d):
        y_ref[:, j * d:(j + 1) * d] = y[j * n1:(j + 1) * n1, :]


def _ffn(h, consts, *, n1, n2):
    bsz, seq, d = h.shape
    j_per = TOKEN_TILE // n1
    return pl.pallas_call(
        functools.partial(_ffn_kernel, n1=n1, d=d, chunk=1024),
        grid=(bsz, n2 // j_per),
        in_specs=[pl.BlockSpec((None, TOKEN_TILE, d), lambda b, t: (b, t, 0)),
                  *[_const_spec(c.shape) for c in consts]],
        out_specs=pl.BlockSpec((None, n1, j_per * d), lambda b, t: (b, 0, t)),
        out_shape=jax.ShapeDtypeStruct((bsz, n1, n2 * d), F32),
        compiler_params=_compiler_params(("parallel", "parallel")),
        name="ffn",
    )(h, *consts)


def _encoder(x, p, *, hyena_cb):
    bsz, seq, d = x.shape
    n1, n2 = _split_factors(seq)
    pool_w = p['w_pool_proj'].shape[0]
    hy_w = p['w_hyena_proj'].shape[0]
    row = lambda v: v.reshape(1, -1).astype(F32)
    tabs = {k: _hi_lo(v) for k, v in _dft_tables(n1, n2).items()}

    xv = x.reshape(bsz, n1, n2 * d)
    a, u3, gate = _inproj(xv, row(p['ln_in_g']), row(p['ln_in_b']), p['w_in'].astype(BF16), row(p['b_in']),
                          n1=n1, n2=n2, widths=(pool_w, HYENA_ORDER * hy_w + hy_w, 2 * d))
    pm = _pool(a, n1=n1, n2=n2)
    kf = _filter_spectra(p, tabs, n1=n1, n2=n2, width=hy_w)
    z = _hyena(u3, p['conv_w'], row(p['conv_b']), kf, p['hyena_bias'], tabs, n1=n1, n2=n2, width=hy_w, cb=hyena_cb)
    merge_consts = [row(p['ln_in_g']), row(p['ln_in_b']), p['w_pool'].astype(BF16), row(p['b_pool']),
                    row(p['pool_scale']), p['w_pool_proj'].astype(BF16), p['w_hyena_proj'].astype(BF16),
                    p['w_o'].astype(BF16), row(p['b_o']), row(p['ln1_g']), row(p['ln1_b'])]
    h1 = _merge(xv, pm, z, gate, merge_consts, n1=n1, n2=n2)
    ffn_consts = [p['w_ff1'].astype(BF16), row(p['b_ff1']), p['w_ff2'].astype(BF16), row(p['b_ff2']),
                  row(p['ln2_g']), row(p['ln2_b'])]
    y = _ffn(h1, ffn_consts, n1=n1, n2=n2)
    return y.reshape(bsz, seq, d)


def kernel(x_prompt, x_sample, ln_in_g, ln_in_b, w_in, b_in, w_pool, b_pool, pool_scale, w_pool_proj, conv_w, conv_b, w_f1, b_f1, freq_f1, w_f2, b_f2, freq_f2, w_f_out, decay_rate, hyena_bias, w_hyena_proj, w_o, b_o, ln1_g, ln1_b, w_ff1, b_ff1, w_ff2, b_ff2, ln2_g, ln2_b):
    layer = dict(w_in=w_in, b_in=b_in, w_pool=w_pool, b_pool=b_pool, pool_scale=pool_scale,
                 w_pool_proj=w_pool_proj, conv_w=conv_w, conv_b=conv_b, w_f1=w_f1, b_f1=b_f1, freq_f1=freq_f1,
                 w_f2=w_f2, b_f2=b_f2, freq_f2=freq_f2, w_f_out=w_f_out, decay_rate=decay_rate,
                 hyena_bias=hyena_bias, w_hyena_proj=w_hyena_proj, w_o=w_o, b_o=b_o, ln1_g=ln1_g, ln1_b=ln1_b,
                 w_ff1=w_ff1, b_ff1=b_ff1, w_ff2=w_ff2, b_ff2=b_ff2, ln2_g=ln2_g, ln2_b=ln2_b)
    assert all(v.shape[0] == DEPTH for v in layer.values())
    p = {k: v[0] for k, v in layer.items()}
    p.update(ln_in_g=ln_in_g, ln_in_b=ln_in_b)
    y_prompt = _encoder(x_prompt, p, hyena_cb=V7X_LANES)
    y_sample = _encoder(x_sample, p, hyena_cb=2 * V7X_LANES)
    return (y_prompt, y_sample)
```

```python
import functools
import math

import numpy as np
import jax
import jax.numpy as jnp
from jax import lax
from jax.experimental import pallas as pl
from jax.experimental.pallas import tpu as pltpu

F32 = jnp.float32
BF16 = jnp.bfloat16

V7X_LANES = 128
V7X_SUBLANES = 8
V7X_VMEM_BYTES = 64 * 1024 * 1024
VMEM_LIMIT_BYTES = V7X_VMEM_BYTES - 8 * 1024 * 1024

POOL_WINDOWS = (2, 4, 8, 16)
POOL_HALO = max(max(w // 2, w - 1 - w // 2) for w in POOL_WINDOWS)
N_DIRS = 2
HYENA_ORDER = 2
POS_BANDS = 16
LN_EPS = 1e-5
L1_EPS = 1e-6
DEPTH = 1
DN_ALPHA = (2.0 * DEPTH) ** 0.25

ROW_PAD = V7X_SUBLANES
TOKEN_TILE = 512


def _compiler_params(semantics):
    return pltpu.CompilerParams(dimension_semantics=semantics, vmem_limit_bytes=VMEM_LIMIT_BYTES)


def _const_spec(shape):
    nd = len(shape)
    return pl.BlockSpec(shape, lambda *_: (0,) * nd, pipeline_mode=pl.Buffered(1))


def _split_factors(seq_len):
    n2 = 64
    assert seq_len % n2 == 0
    return seq_len // n2, n2


def _dft_tables(n1, n2):
    seq = n1 * n2
    i1 = np.arange(n1)
    th = np.pi * np.outer(i1, i1) / n1
    f1 = np.zeros((2 * n1, n1))
    f1[:n1] = np.cos(th)
    f1[n1] = (-1.0) ** i1
    f1[n1 + 1:] = -np.sin(th[1:])
    g = np.zeros((n1, 2 * n1))
    g[:, :n1] = np.cos(th.T)
    g[:, n1] = (-1.0) ** i1
    g[:, n1 + 1:] = -np.sin(th.T[:, 1:])
    i2 = np.arange(n2)
    base = 2 * np.pi * np.outer(i2, i2) / n2
    m = np.zeros((n1, 2 * n2, 2 * n2))
    mi = np.zeros((n1, 2 * n2, 2 * n2))
    for k in range(1, n1):
        phi = base + 2 * np.pi * k * i2[None, :] / (2 * seq)
        c, s = np.cos(phi), np.sin(phi)
        m[k] = np.block([[c, s], [-s, c]])
        mi[k] = np.block([[c.T, -s.T], [s.T, c.T]]) / seq
    phin = base + np.pi * i2[None, :] / n2
    z = np.zeros((n2, n2))
    ms = np.block([[np.cos(base), z], [-np.sin(base), z], [z, np.cos(phin)], [z, -np.sin(phin)]])
    mis = np.block([[np.cos(base).T, -np.sin(base).T, z, z],
                    [z, z, np.cos(phin).T, -np.sin(phin).T]]) / (2 * seq)
    return dict(f1=f1, g=g, m=m, mi=mi, ms=ms, mis=mis)


def _hi_lo(x):
    hi = jnp.asarray(x, F32).astype(BF16)
    lo = (jnp.asarray(x, F32) - hi.astype(F32)).astype(BF16)
    return hi, lo


def _perm_positions(n1, n2):
    return (np.arange(n2)[:, None] + n2 * np.arange(n1)[None, :]).reshape(-1)


def _positional_features(seq_len, n1, n2):
    pos = _perm_positions(n1, n2).astype(np.float64)
    t = pos / (seq_len - 1)
    w = 2.0 * math.pi * pos / seq_len
    f = np.linspace(1e-4, POS_BANDS - 1, POS_BANDS)
    z = np.concatenate([t[:, None], np.cos(f[None] * w[:, None]), -np.sin(f[None] * w[:, None])], axis=-1)
    zp = np.zeros((seq_len, 40))
    zp[:, :z.shape[1]] = z
    return zp, t[:, None]


def _layer_norm(x, g, b):
    mu = jnp.mean(x, axis=-1, keepdims=True)
    xc = x - mu
    var = jnp.mean(xc * xc, axis=-1, keepdims=True)
    return xc * lax.rsqrt(var + LN_EPS) * g + b


def _dot(a, b):
    return jnp.dot(a, b, preferred_element_type=F32)


def _dot3(a_hi, a_lo, b):
    b_hi = b.astype(BF16)
    b_lo = (b - b_hi.astype(F32)).astype(BF16)
    return _dot(a_hi, b_hi) + (_dot(a_hi, b_lo) + _dot(a_lo, b_hi))


def _store_slabs(dst_ref, row0, val):
    rows = val.shape[0]
    for j in range(dst_ref.shape[0]):
        dst_ref[j, pl.ds(row0, rows), :] = val[:, j * V7X_LANES:(j + 1) * V7X_LANES]


def _gather_pair(src_ref, start_a, start_b, count, pitch):
    cols = []
    for j in range(src_ref.shape[0]):
        a = src_ref[j, pl.ds(start_a, count, stride=pitch), :]
        b = src_ref[j, pl.ds(start_b, count, stride=pitch), :]
        cols.append(jnp.concatenate([a, b], axis=0))
    return cols[0] if len(cols) == 1 else jnp.concatenate(cols, axis=1)


def _complex_mul(x, k, half):
    xr, xi = x[:half], x[half:]
    kr, ki = k[:half], k[half:]
    return jnp.concatenate([xr * kr - xi * ki, xr * ki + xi * kr], axis=0)


def _row_permutation(n1, j_per):
    i, j = np.meshgrid(np.arange(n1), np.arange(j_per), indexing="ij")
    p = np.zeros((n1 * j_per, n1 * j_per), np.float32)
    p[(j * n1 + i).ravel(), (i * j_per + j).ravel()] = 1.0
    return p


def _inproj_kernel(x_ref, perm_ref, g_ref, b_ref, w_ref, bias_ref, a_ref, u_ref, gate_ref, h_scr, *, chunk):
    n1, j_per, d = x_ref.shape
    for i in range(0, n1, V7X_SUBLANES):
        x = x_ref[i:i + V7X_SUBLANES].reshape(V7X_SUBLANES * j_per, d)
        h_scr[i * j_per:(i + V7X_SUBLANES) * j_per, :] = _layer_norm(x, g_ref[...], b_ref[...]).astype(BF16)
    h = _dot(perm_ref[...], h_scr[...]).astype(BF16)
    wa, wu = a_ref.shape[1], u_ref.shape[1]
    total = w_ref.shape[1]
    for c in range(0, total, chunk):
        y = _dot(h, w_ref[:, c:c + chunk]) + bias_ref[:, c:c + chunk]
        if c < wa:
            a_ref[:, c:c + chunk] = y
        elif c < wa + wu:
            u_ref[:, c - wa:c - wa + chunk] = y
        else:
            gate_ref[:, c - wa - wu:c - wa - wu + chunk] = jax.nn.sigmoid(y)


def _inproj(xv, perm, ln_g, ln_b, w_in, b_in, *, n1, n2, widths):
    bsz = xv.shape[0]
    d = w_in.shape[0]
    seq = n1 * n2
    j_per = TOKEN_TILE // n1
    wa, wu, wg = widths
    chunk = 512
    assert wa % chunk == 0 and wu % chunk == 0 and wg % chunk == 0 and n2 % j_per == 0
    out_shapes = tuple(jax.ShapeDtypeStruct((bsz, seq, w), F32) for w in widths)
    tok = lambda w: pl.BlockSpec((None, TOKEN_TILE, w), lambda b, t: (b, t, 0))
    return pl.pallas_call(
        functools.partial(_inproj_kernel, chunk=chunk),
        grid=(bsz, n2 // j_per),
        in_specs=[
            pl.BlockSpec((None, n1, j_per, d), lambda b, t: (b, 0, t, 0)),
            _const_spec(perm.shape), _const_spec((1, d)), _const_spec((1, d)),
            _const_spec(w_in.shape), _const_spec(b_in.shape),
        ],
        out_specs=[tok(wa), tok(wu), tok(wg)],
        out_shape=out_shapes,
        scratch_shapes=[pltpu.VMEM((TOKEN_TILE, d), BF16)],
        compiler_params=_compiler_params(("parallel", "parallel")),
        name="inproj",
    )(xv, perm, ln_g, ln_b, w_in, b_in)


def _pool_kernel(a_ref, o_ref, e_ref, *, n1, n2):
    seq = n1 * n2
    halo = POOL_HALO
    rows = lax.broadcasted_iota(jnp.int32, (n1, V7X_LANES), 0)

    def body(win):
        lo = win // 2
        hi = win - 1 - lo

        def copy(i, c):
            r = pl.multiple_of(i * n1, n1)
            e_ref[pl.ds(halo * n1 + r, n1), :] = a_ref[pl.ds(r, n1), :]
            return c

        lax.fori_loop(0, n2, copy, 0, unroll=4)
        for m in range(1, lo + 1):
            blk = a_ref[pl.ds((n2 - m) * n1 - 1, n1), :]
            e_ref[pl.ds((halo - m) * n1, n1), :] = jnp.where(rows == 0, 0.0, blk)
        for m in range(hi):
            blk = a_ref[pl.ds(m * n1 + 1, n1), :]
            e_ref[pl.ds((halo + n2 + m) * n1, n1), :] = jnp.where(rows == n1 - 1, 0.0, blk)

        def step(i, c):
            r = pl.multiple_of(i * n1, n1)
            acc = e_ref[pl.ds((halo - lo) * n1 + r, n1), :]
            for dlt in range(-lo + 1, hi + 1):
                acc = acc + e_ref[pl.ds((halo + dlt) * n1 + r, n1), :]
            pos = rows * n2 + i
            cnt = jnp.minimum(pos + (win - lo), seq) - jnp.maximum(pos - lo, 0)
            o_ref[pl.ds(r, n1), :] = acc / cnt.astype(F32) - a_ref[pl.ds(r, n1), :]
            return c

        lax.fori_loop(0, n2, step, 0, unroll=2)

    grp = pl.program_id(1)
    for gi, win in enumerate(POOL_WINDOWS):
        pl.when(grp == gi)(functools.partial(body, win))


def _pool(a, *, n1, n2):
    bsz, seq, width = a.shape
    ngrp = len(POOL_WINDOWS)
    assert width == ngrp * V7X_LANES
    spec = pl.BlockSpec((None, seq, V7X_LANES), lambda b, g: (b, 0, g))
    return pl.pallas_call(
        functools.partial(_pool_kernel, n1=n1, n2=n2),
        grid=(bsz, ngrp),
        in_specs=[spec],
        out_specs=spec,
        out_shape=jax.ShapeDtypeStruct(a.shape, F32),
        scratch_shapes=[pltpu.VMEM(((n2 + 2 * POOL_HALO) * n1, V7X_LANES), F32)],
        compiler_params=_compiler_params(("parallel", "parallel")),
        name="pool",
    )(a)


def _filter_kernel(z_ref, t_ref, w1_ref, b1_ref, q1_ref, w2_ref, b2_ref, q2_ref, wo_ref, dec_ref,
                   f1h_ref, f1l_ref, mh_ref, ml_ref, msh_ref, msl_ref,
                   kf_ref, hid_ref, h_ref, s1_ref, *, n1, n2):
    seq = n1 * n2
    pitch = 2 * n1 + ROW_PAD
    hp = lax.Precision.HIGHEST
    nset = h_ref.shape[0]
    rb = 256

    def mlp(i, c):
        r = pl.multiple_of(i * rb, rb)
        z = z_ref[pl.ds(r, rb), :]
        h = jnp.sin(q1_ref[...] * (jnp.dot(z, w1_ref[...], precision=hp, preferred_element_type=F32) + b1_ref[...]))
        h = jnp.sin(q2_ref[...] * (jnp.dot(h, w2_ref[...], precision=hp, preferred_element_type=F32) + b2_ref[...]))
        hid_ref[pl.ds(r, rb), :] = h
        return c

    lax.fori_loop(0, seq // rb, mlp, 0, unroll=4)

    sums = []
    for s in range(nset):
        rate = jnp.abs(dec_ref[s:s + 1, :])

        def gen(i, acc, s=s, rate=rate):
            r = pl.multiple_of(i * rb, rb)
            h = jnp.dot(hid_ref[pl.ds(r, rb), :], wo_ref[s], precision=hp, preferred_element_type=F32)
            h = h * jnp.exp(-t_ref[pl.ds(r, rb), :] * rate)
            if s % N_DIRS == 1:
                row = lax.broadcasted_iota(jnp.int32, h.shape, 0) + r
                h = jnp.where(row == 0, 0.0, h)
            h_ref[s, pl.ds(r, rb), :] = h
            return acc + jnp.sum(jnp.abs(h), axis=0, keepdims=True)

        sums.append(lax.fori_loop(0, seq // rb, gen, jnp.zeros((1, h_ref.shape[2]), F32), unroll=4))

    for s in range(nset):
        o, dr = divmod(s, N_DIRS)
        denom = sums[o * N_DIRS] + sums[o * N_DIRS + 1] + L1_EPS

        def stage1(i, c, s=s, denom=denom):
            r = pl.multiple_of(i * n1, n1)
            blk = h_ref[s, pl.ds(r, n1), :] / denom
            _store_slabs(s1_ref, pl.multiple_of(i * pitch, V7X_SUBLANES), _dot3(f1h_ref[...], f1l_ref[...], blk))
            return c

        lax.fori_loop(0, n2, stage1, 0, unroll=8)

        def emit(row0, x, half, dr=dr, o=o):
            if dr == 0:
                kf_ref[o, pl.ds(row0, 2 * half), :] = x
            else:
                sign = jnp.where(lax.broadcasted_iota(jnp.int32, x.shape, 0) < half, 1.0, -1.0)
                kf_ref[o, pl.ds(row0, 2 * half), :] += sign * x

        b0 = _gather_pair(s1_ref, 0, n1, n2, pitch)
        x0 = _dot3(msh_ref[...], msl_ref[...], b0)
        emit(0, x0[:2 * n2], n2)
        emit(2 * n2, x0[2 * n2:], n2)

        def stage2(k, c, emit=emit):
            bk = _gather_pair(s1_ref, k, n1 + k, n2, pitch)
            emit(pl.multiple_of((k + 1) * 2 * n2, 2 * n2), _dot3(mh_ref[k], ml_ref[k], bk), n2)
            return c

        lax.fori_loop(1, n1, stage2, 0, unroll=7)


def _filter_spectra(p, tabs, *, n1, n2, width):
    seq = n1 * n2
    cb = V7X_LANES
    nset = N_DIRS * HYENA_ORDER
    nrow = 2 * n2 * (n1 + 1)
    pitch = 2 * n1 + ROW_PAD
    zfeat, tcol = _positional_features(seq, n1, n2)
    hid = p['w_f2'].shape[0]
    w1 = jnp.zeros((zfeat.shape[1], hid), F32).at[:p['w_f1'].shape[0]].set(p['w_f1'])
    wo = p['w_f_out'].reshape(hid, nset, width).transpose(1, 0, 2)
    row = lambda v: v.reshape(1, -1)
    operands = [jnp.asarray(zfeat, F32), jnp.asarray(tcol, F32), w1, row(p['b_f1']), row(p['freq_f1']),
                p['w_f2'], row(p['b_f2']), row(p['freq_f2']), wo, p['decay_rate'],
                *tabs['f1'], *tabs['m'], *tabs['ms']]
    in_specs = [_const_spec(o.shape) for o in operands]
    in_specs[8] = pl.BlockSpec((nset, hid, cb), lambda c: (0, 0, c))
    in_specs[9] = pl.BlockSpec((nset, cb), lambda c: (0, c))
    return pl.pallas_call(
        functools.partial(_filter_kernel, n1=n1, n2=n2),
        grid=(width // cb,),
        in_specs=in_specs,
        out_specs=pl.BlockSpec((HYENA_ORDER, nrow, cb), lambda c: (0, 0, c)),
        out_shape=jax.ShapeDtypeStruct((HYENA_ORDER, nrow, width), F32),
        scratch_shapes=[pltpu.VMEM((seq, hid), F32), pltpu.VMEM((nset, seq, cb), F32),
                        pltpu.VMEM((cb // V7X_LANES, n2 * pitch, V7X_LANES), F32)],
        compiler_params=_compiler_params(("parallel",)),
        name="filter_spectra",
    )(*operands)


def _hyena_kernel(x1_ref, x2_ref, v_ref, cw1_ref, cw2_ref, cwv_ref, cb1_ref, cb2_ref, cbv_ref,
                  kf_ref, hb_ref, f1_ref, g_ref, m_ref, mi_ref, ms_ref, mis_ref,
                  z_ref, u_scr, w_scr, g1_scr, g2_scr, s1_ref, s2_ref, *, n1, n2, unroll_blocks, unroll_slots):
    pitch1 = 2 * n1 + ROW_PAD
    pitch2 = 2 * n2 + ROW_PAD
    cbw = z_ref.shape[1]
    rows = lax.broadcasted_iota(jnp.int32, (n1, cbw), 0)

    def short_conv(src, w_ref, b_ref, dst):
        def blk(r, prev, nxt):
            dst[pl.ds(r, n1), :] = (prev * w_ref[0:1, :] + src[pl.ds(r, n1), :] * w_ref[1:2, :]
                                    + nxt * w_ref[2:3, :] + b_ref[...])

        blk(0, jnp.where(rows == 0, 0.0, src[pl.ds((n2 - 1) * n1 - 1, n1), :]), src[pl.ds(n1, n1), :])
        last = (n2 - 1) * n1
        blk(last, src[pl.ds(last - n1, n1), :], jnp.where(rows == n1 - 1, 0.0, src[pl.ds(1, n1), :]))

        def mid(i, c):
            r = pl.multiple_of(i * n1, n1)
            blk(r, src[pl.ds(pl.multiple_of(r - n1, n1), n1), :], src[pl.ds(pl.multiple_of(r + n1, n1), n1), :])
            return c

        lax.fori_loop(1, n2 - 1, mid, 0, unroll=2)

    def first(u_in, i, c):
        u = u_in[pl.ds(pl.multiple_of(i * n1, n1), n1), :]
        _store_slabs(s1_ref, pl.multiple_of(i * pitch1, V7X_SUBLANES), _dot(f1_ref[...], u.astype(BF16)))
        return c

    def middle(o):
        b0 = _gather_pair(s1_ref, 0, n1, n2, pitch1).astype(BF16)
        x0 = _dot(ms_ref[...], b0)
        y0 = jnp.concatenate([_complex_mul(x0[:2 * n2], kf_ref[o, 0:2 * n2, :], n2),
                              _complex_mul(x0[2 * n2:], kf_ref[o, 2 * n2:4 * n2, :], n2)], axis=0)
        _store_slabs(s2_ref, 0, _dot(mis_ref[...], y0.astype(BF16)))

        def slot(k, c):
            bk = _gather_pair(s1_ref, k, n1 + k, n2, pitch1).astype(BF16)
            xk = _dot(m_ref[k], bk)
            kf = kf_ref[o, pl.ds(pl.multiple_of((k + 1) * 2 * n2, 2 * n2), 2 * n2), :]
            yk = _complex_mul(xk, kf, n2).astype(BF16)
            _store_slabs(s2_ref, pl.multiple_of(k * pitch2, V7X_SUBLANES), _dot(mi_ref[k], yk))
            return c

        lax.fori_loop(1, n1, slot, 0, unroll=unroll_slots)

    def finish(o, gate_scr, u_in, z_out, i, c):
        dk = _gather_pair(s2_ref, i, n2 + i, n1, pitch2).astype(BF16)
        r = pl.multiple_of(i * n1, n1)
        conv = _dot(g_ref[...], dk) + u_in[pl.ds(r, n1), :] * hb_ref[o:o + 1, :]
        z_out[pl.ds(r, n1), :] = gate_scr[pl.ds(r, n1), :] * conv
        return c

    def blocks(fn):
        lax.fori_loop(0, n2, fn, 0, unroll=unroll_blocks)

    short_conv(v_ref, cwv_ref, cbv_ref, u_scr)
    short_conv(x1_ref, cw1_ref, cb1_ref, g1_scr)
    short_conv(x2_ref, cw2_ref, cb2_ref, g2_scr)
    blocks(functools.partial(first, u_scr))
    middle(0)
    blocks(functools.partial(finish, 0, g1_scr, u_scr, w_scr))
    blocks(functools.partial(first, w_scr))
    middle(1)
    blocks(functools.partial(finish, 1, g2_scr, w_scr, z_ref))


def _hyena_config(n1, n2):
    seq = n1 * n2
    nrow = 2 * n2 * (n1 + 1)

    def vmem_bytes(cb):
        windows = 2 * 4 * seq * cb
        scratch = 4 * seq * cb + n2 * (2 * n1 + ROW_PAD) * cb + n1 * (2 * n2 + ROW_PAD) * cb
        tables = 2 * (2 * n1 * (2 * n2) ** 2 + 2 * 4 * n2 * 2 * n2 + 2 * 2 * n1 * n1) // 4
        return 4 * (windows + scratch + HYENA_ORDER * nrow * cb + tables)

    cb = next(c for c in (2 * V7X_LANES, V7X_LANES) if vmem_bytes(c) <= VMEM_LIMIT_BYTES)
    slots = n1 - 1
    divisors = [u for u in range(8, 25) if slots % u == 0]
    return cb, 16, (max(divisors) if divisors else 16)


def _hyena(u3, conv_w, conv_b, kf, hyena_bias, tabs, *, n1, n2, width):
    cb, unroll_blocks, unroll_slots = _hyena_config(n1, n2)
    bsz, seq, _ = u3.shape
    ncb = width // cb
    nrow = kf.shape[1]
    pitch1 = 2 * n1 + ROW_PAD
    pitch2 = 2 * n2 + ROW_PAD
    nslab = cb // V7X_LANES
    seq_spec = lambda part: pl.BlockSpec((None, seq, cb), lambda c, b: (b, 0, part * ncb + c))
    tap_spec = lambda part: pl.BlockSpec((conv_w.shape[0], cb), lambda c, b: (0, part * ncb + c))
    bias_spec = lambda part: pl.BlockSpec((1, cb), lambda c, b: (0, part * ncb + c))
    consts = [tabs['f1'][0], tabs['g'][0], tabs['m'][0], tabs['mi'][0], tabs['ms'][0], tabs['mis'][0]]
    return pl.pallas_call(
        functools.partial(_hyena_kernel, n1=n1, n2=n2, unroll_blocks=unroll_blocks, unroll_slots=unroll_slots),
        grid=(ncb, bsz),
        in_specs=[seq_spec(0), seq_spec(1), seq_spec(2),
                  tap_spec(0), tap_spec(1), tap_spec(2), bias_spec(0), bias_spec(1), bias_spec(2),
                  pl.BlockSpec((HYENA_ORDER, nrow, cb), lambda c, b: (0, 0, c), pipeline_mode=pl.Buffered(1)),
                  pl.BlockSpec((HYENA_ORDER, cb), lambda c, b: (0, c)),
                  *[_const_spec(t.shape) for t in consts]],
        out_specs=pl.BlockSpec((None, seq, cb), lambda c, b: (b, 0, c)),
        out_shape=jax.ShapeDtypeStruct((bsz, seq, width), F32),
        scratch_shapes=[*[pltpu.VMEM((seq, cb), F32)] * 4,
                        pltpu.VMEM((nslab, n2 * pitch1, V7X_LANES), F32),
                        pltpu.VMEM((nslab, n1 * pitch2, V7X_LANES), F32)],
        compiler_params=_compiler_params(("parallel", "parallel")),
        name="hyena",
    )(u3, u3, u3, conv_w, conv_w, conv_w, conv_b, conv_b, conv_b, kf, hyena_bias, *consts)


def _merge_kernel(x_ref, permt_ref, pm_ref, z_ref, gate_ref, lng_ref, lnb_ref, wp_ref, bp_ref, ps_ref, wpp_ref,
                  whp_ref, wo_ref, bo_ref, g1_ref, b1_ref, h_ref):
    tm, d = z_ref.shape
    ngrp = wp_ref.shape[0]
    gd = wp_ref.shape[1]
    pm = pm_ref[...].astype(BF16)
    p = jnp.concatenate([_dot(pm[:, g * gd:(g + 1) * gd], wp_ref[g]) for g in range(ngrp)], axis=1)
    p = (p + bp_ref[...]) * ps_ref[...]
    ya = _dot(p.astype(BF16), wpp_ref[...])
    yb = _dot(z_ref[...].astype(BF16), whp_ref[...])
    m = gate_ref[:, :d] * ya + gate_ref[:, d:] * yb
    mo = _dot(m.astype(BF16), wo_ref[...]) + bo_ref[...]
    hi = mo.astype(BF16)
    rest = mo - hi.astype(F32)
    mid = rest.astype(BF16)
    lo = (rest - mid.astype(F32)).astype(BF16)
    mo = (_dot(permt_ref[...], hi) + _dot(permt_ref[...], mid)) + _dot(permt_ref[...], lo)
    h0 = _layer_norm(x_ref[...].reshape(tm, d), lng_ref[...], lnb_ref[...])
    h_ref[...] = _layer_norm(DN_ALPHA * h0 + mo, g1_ref[...], b1_ref[...]).reshape(h_ref.shape)


def _merge(xv, permt, pm, z, gate, consts, *, n1, n2):
    bsz, seq, d = z.shape
    j_per = TOKEN_TILE // n1
    tok = lambda w: pl.BlockSpec((None, TOKEN_TILE, w), lambda b, t: (b, t, 0))
    nat = pl.BlockSpec((None, n1, j_per, d), lambda b, t: (b, 0, t, 0))
    return pl.pallas_call(
        _merge_kernel,
        grid=(bsz, n2 // j_per),
        in_specs=[nat, _const_spec(permt.shape), tok(pm.shape[2]), tok(d), tok(gate.shape[2]),
                  *[_const_spec(c.shape) for c in consts]],
        out_specs=nat,
        out_shape=jax.ShapeDtypeStruct(xv.shape, F32),
        compiler_params=_compiler_params(("parallel", "parallel")),
        name="merge",
    )(xv, permt, pm, z, gate, *consts)


def _ffn_kernel(h_ref, w1_ref, b1_ref, w2_ref, b2_ref, g_ref, b_ref, y_ref, *, chunk):
    h = h_ref[...]
    hb = h.astype(BF16)
    dff = w1_ref.shape[1]
    acc = jnp.zeros(h.shape, F32)
    for c in range(0, dff, chunk):
        u = jnp.maximum(_dot(hb, w1_ref[:, c:c + chunk]) + b1_ref[:, c:c + chunk], 0.0)
        acc = acc + _dot((u * u).astype(BF16), w2_ref[c:c + chunk, :])
    y_ref[...] = _layer_norm(DN_ALPHA * h + acc + b2_ref[...], g_ref[...], b_ref[...])


def _ffn(h, consts):
    bsz, seq, d = h.shape
    tok = pl.BlockSpec((None, TOKEN_TILE, d), lambda b, t: (b, t, 0))
    return pl.pallas_call(
        functools.partial(_ffn_kernel, chunk=1024),
        grid=(bsz, seq // TOKEN_TILE),
        in_specs=[tok, *[_const_spec(c.shape) for c in consts]],
        out_specs=tok,
        out_shape=jax.ShapeDtypeStruct(h.shape, F32),
        compiler_params=_compiler_params(("parallel", "parallel")),
        name="ffn",
    )(h, *consts)


def _encoder(x, p):
    bsz, seq, d = x.shape
    n1, n2 = _split_factors(seq)
    pool_w = p['w_pool_proj'].shape[0]
    hy_w = p['w_hyena_proj'].shape[0]
    row = lambda v: v.reshape(1, -1).astype(F32)
    tabs = {k: _hi_lo(v) for k, v in _dft_tables(n1, n2).items()}

    xv = x.reshape(bsz, n1, n2, d)
    perm = _row_permutation(n1, TOKEN_TILE // n1)
    a, u3, gate = _inproj(xv, jnp.asarray(perm, BF16), row(p['ln_in_g']), row(p['ln_in_b']),
                          p['w_in'].astype(BF16), row(p['b_in']),
                          n1=n1, n2=n2, widths=(pool_w, HYENA_ORDER * hy_w + hy_w, 2 * d))
    pm = _pool(a, n1=n1, n2=n2)
    kf = _filter_spectra(p, tabs, n1=n1, n2=n2, width=hy_w)
    z = _hyena(u3, p['conv_w'], row(p['conv_b']), kf, p['hyena_bias'], tabs, n1=n1, n2=n2, width=hy_w)
    merge_consts = [row(p['ln_in_g']), row(p['ln_in_b']), p['w_pool'].astype(BF16), row(p['b_pool']),
                    row(p['pool_scale']), p['w_pool_proj'].astype(BF16), p['w_hyena_proj'].astype(BF16),
                    p['w_o'].astype(BF16), row(p['b_o']), row(p['ln1_g']), row(p['ln1_b'])]
    h1 = _merge(xv, jnp.asarray(perm.T, BF16), pm, z, gate, merge_consts, n1=n1, n2=n2)
    ffn_consts = [p['w_ff1'].astype(BF16), row(p['b_ff1']), p['w_ff2'].astype(BF16), row(p['b_ff2']),
                  row(p['ln2_g']), row(p['ln2_b'])]
    return _ffn(h1.reshape(bsz, seq, d), ffn_consts)


def kernel(x_prompt, x_sample, ln_in_g, ln_in_b, w_in, b_in, w_pool, b_pool, pool_scale, w_pool_proj, conv_w, conv_b, w_f1, b_f1, freq_f1, w_f2, b_f2, freq_f2, w_f_out, decay_rate, hyena_bias, w_hyena_proj, w_o, b_o, ln1_g, ln1_b, w_ff1, b_ff1, w_ff2, b_ff2, ln2_g, ln2_b):
    layer = dict(w_in=w_in, b_in=b_in, w_pool=w_pool, b_pool=b_pool, pool_scale=pool_scale,
                 w_pool_proj=w_pool_proj, conv_w=conv_w, conv_b=conv_b, w_f1=w_f1, b_f1=b_f1, freq_f1=freq_f1,
                 w_f2=w_f2, b_f2=b_f2, freq_f2=freq_f2, w_f_out=w_f_out, decay_rate=decay_rate,
                 hyena_bias=hyena_bias, w_hyena_proj=w_hyena_proj, w_o=w_o, b_o=b_o, ln1_g=ln1_g, ln1_b=ln1_b,
                 w_ff1=w_ff1, b_ff1=b_ff1, w_ff2=w_ff2, b_ff2=b_ff2, ln2_g=ln2_g, ln2_b=ln2_b)
    assert all(v.shape[0] == DEPTH for v in layer.values())
    p = {k: v[0] for k, v in layer.items()}
    p.update(ln_in_g=ln_in_g, ln_in_b=ln_in_b)
    y_prompt = _encoder(x_prompt, p)
    y_sample = _encoder(x_sample, p)
    return (y_prompt, y_sample)
```

```python
import functools
import math

import numpy as np
import jax
import jax.numpy as jnp
from jax import lax
from jax.experimental import pallas as pl
from jax.experimental.pallas import tpu as pltpu

F32 = jnp.float32
BF16 = jnp.bfloat16

V7X_LANES = 128
V7X_SUBLANES = 8
V7X_VMEM_BYTES = 64 * 1024 * 1024
VMEM_LIMIT_BYTES = V7X_VMEM_BYTES - 8 * 1024 * 1024

POOL_WINDOWS = (2, 4, 8, 16)
POOL_HALO = max(max(w // 2, w - 1 - w // 2) for w in POOL_WINDOWS)
N_DIRS = 2
HYENA_ORDER = 2
POS_BANDS = 16
LN_EPS = 1e-5
L1_EPS = 1e-6
DEPTH = 1
DN_ALPHA = (2.0 * DEPTH) ** 0.25

ROW_PAD = V7X_SUBLANES
TOKEN_TILE = 512


def _compiler_params(semantics):
    return pltpu.CompilerParams(dimension_semantics=semantics, vmem_limit_bytes=VMEM_LIMIT_BYTES)


def _const_spec(shape):
    nd = len(shape)
    return pl.BlockSpec(shape, lambda *_: (0,) * nd, pipeline_mode=pl.Buffered(1))


def _split_factors(seq_len):
    n2 = 64
    assert seq_len % n2 == 0
    return seq_len // n2, n2


def _dft_tables(n1, n2):
    seq = n1 * n2
    i1 = np.arange(n1)
    th = np.pi * np.outer(i1, i1) / n1
    f1 = np.zeros((2 * n1, n1))
    f1[:n1] = np.cos(th)
    f1[n1] = (-1.0) ** i1
    f1[n1 + 1:] = -np.sin(th[1:])
    g = np.zeros((n1, 2 * n1))
    g[:, :n1] = np.cos(th.T)
    g[:, n1] = (-1.0) ** i1
    g[:, n1 + 1:] = -np.sin(th.T[:, 1:])
    i2 = np.arange(n2)
    base = 2 * np.pi * np.outer(i2, i2) / n2
    m = np.zeros((n1, 2 * n2, 2 * n2))
    mi = np.zeros((n1, 2 * n2, 2 * n2))
    for k in range(1, n1):
        phi = base + 2 * np.pi * k * i2[None, :] / (2 * seq)
        c, s = np.cos(phi), np.sin(phi)
        m[k] = np.block([[c, s], [-s, c]])
        mi[k] = np.block([[c.T, -s.T], [s.T, c.T]]) / seq
    phin = base + np.pi * i2[None, :] / n2
    z = np.zeros((n2, n2))
    ms = np.block([[np.cos(base), z], [-np.sin(base), z], [z, np.cos(phin)], [z, -np.sin(phin)]])
    mis = np.block([[np.cos(base).T, -np.sin(base).T, z, z],
                    [z, z, np.cos(phin).T, -np.sin(phin).T]]) / (2 * seq)
    return dict(f1=f1, g=g, m=m, mi=mi, ms=ms, mis=mis)


def _hi_lo(x):
    hi = jnp.asarray(x, F32).astype(BF16)
    lo = (jnp.asarray(x, F32) - hi.astype(F32)).astype(BF16)
    return hi, lo


def _perm_positions(n1, n2):
    return (np.arange(n2)[:, None] + n2 * np.arange(n1)[None, :]).reshape(-1)


def _positional_features(seq_len, n1, n2):
    pos = _perm_positions(n1, n2).astype(np.float64)
    t = pos / (seq_len - 1)
    w = 2.0 * math.pi * pos / seq_len
    f = np.linspace(1e-4, POS_BANDS - 1, POS_BANDS)
    z = np.concatenate([t[:, None], np.cos(f[None] * w[:, None]), -np.sin(f[None] * w[:, None])], axis=-1)
    zp = np.zeros((seq_len, 40))
    zp[:, :z.shape[1]] = z
    return zp, t[:, None]


def _layer_norm(x, g, b):
    mu = jnp.mean(x, axis=-1, keepdims=True)
    xc = x - mu
    var = jnp.mean(xc * xc, axis=-1, keepdims=True)
    return xc * lax.rsqrt(var + LN_EPS) * g + b


def _dot(a, b):
    return jnp.dot(a, b, preferred_element_type=F32)


def _dot3(a_hi, a_lo, b):
    b_hi = b.astype(BF16)
    b_lo = (b - b_hi.astype(F32)).astype(BF16)
    return _dot(a_hi, b_hi) + (_dot(a_hi, b_lo) + _dot(a_lo, b_hi))


def _store_slabs(dst_ref, row0, val):
    rows = val.shape[0]
    for j in range(dst_ref.shape[0]):
        dst_ref[j, pl.ds(row0, rows), :] = val[:, j * V7X_LANES:(j + 1) * V7X_LANES]


def _gather_pair(src_ref, start_a, start_b, count, pitch):
    cols = []
    for j in range(src_ref.shape[0]):
        a = src_ref[j, pl.ds(start_a, count, stride=pitch), :]
        b = src_ref[j, pl.ds(start_b, count, stride=pitch), :]
        cols.append(jnp.concatenate([a, b], axis=0))
    return cols[0] if len(cols) == 1 else jnp.concatenate(cols, axis=1)


def _complex_mul(x, k, half):
    xr, xi = x[:half], x[half:]
    kr, ki = k[:half], k[half:]
    return jnp.concatenate([xr * kr - xi * ki, xr * ki + xi * kr], axis=0)


def _row_permutation(n1, j_per):
    i, j = np.meshgrid(np.arange(n1), np.arange(j_per), indexing="ij")
    p = np.zeros((n1 * j_per, n1 * j_per), np.float32)
    p[(j * n1 + i).ravel(), (i * j_per + j).ravel()] = 1.0
    return p


def _inproj_kernel(x_ref, perm_ref, g_ref, b_ref, w_ref, bias_ref, a_ref, u_ref, gate_ref, h_scr, *, chunk):
    n1, j_per, d = x_ref.shape
    for i in range(0, n1, V7X_SUBLANES):
        x = x_ref[i:i + V7X_SUBLANES].reshape(V7X_SUBLANES * j_per, d)
        h_scr[i * j_per:(i + V7X_SUBLANES) * j_per, :] = _layer_norm(x, g_ref[...], b_ref[...]).astype(BF16)
    h = _dot(perm_ref[...], h_scr[...]).astype(BF16)
    wa, wu = a_ref.shape[1], u_ref.shape[1]
    total = w_ref.shape[1]
    for c in range(0, total, chunk):
        y = _dot(h, w_ref[:, c:c + chunk]) + bias_ref[:, c:c + chunk]
        if c < wa:
            a_ref[:, c:c + chunk] = y
        elif c < wa + wu:
            u_ref[:, c - wa:c - wa + chunk] = y
        else:
            gate_ref[:, c - wa - wu:c - wa - wu + chunk] = jax.nn.sigmoid(y)


def _inproj(xv, perm, ln_g, ln_b, w_in, b_in, *, n1, n2, widths):
    bsz = xv.shape[0]
    d = w_in.shape[0]
    seq = n1 * n2
    j_per = TOKEN_TILE // n1
    wa, wu, wg = widths
    chunk = 512
    assert wa % chunk == 0 and wu % chunk == 0 and wg % chunk == 0 and n2 % j_per == 0
    out_shapes = tuple(jax.ShapeDtypeStruct((bsz, seq, w), F32) for w in widths)
    tok = lambda w: pl.BlockSpec((None, TOKEN_TILE, w), lambda b, t: (b, t, 0))
    return pl.pallas_call(
        functools.partial(_inproj_kernel, chunk=chunk),
        grid=(bsz, n2 // j_per),
        in_specs=[
            pl.BlockSpec((None, n1, j_per, d), lambda b, t: (b, 0, t, 0)),
            _const_spec(perm.shape), _const_spec((1, d)), _const_spec((1, d)),
            _const_spec(w_in.shape), _const_spec(b_in.shape),
        ],
        out_specs=[tok(wa), tok(wu), tok(wg)],
        out_shape=out_shapes,
        scratch_shapes=[pltpu.VMEM((TOKEN_TILE, d), BF16)],
        compiler_params=_compiler_params(("parallel", "parallel")),
        name="inproj",
    )(xv, perm, ln_g, ln_b, w_in, b_in)


def _pool_kernel(a_ref, o_ref, e_ref, *, n1, n2):
    seq = n1 * n2
    halo = POOL_HALO
    rows = lax.broadcasted_iota(jnp.int32, (n1, V7X_LANES), 0)

    def body(win):
        lo = win // 2
        hi = win - 1 - lo

        def copy(i, c):
            r = pl.multiple_of(i * n1, n1)
            e_ref[pl.ds(halo * n1 + r, n1), :] = a_ref[pl.ds(r, n1), :]
            return c

        lax.fori_loop(0, n2, copy, 0, unroll=4)
        for m in range(1, lo + 1):
            blk = a_ref[pl.ds((n2 - m) * n1 - 1, n1), :]
            e_ref[pl.ds((halo - m) * n1, n1), :] = jnp.where(rows == 0, 0.0, blk)
        for m in range(hi):
            blk = a_ref[pl.ds(m * n1 + 1, n1), :]
            e_ref[pl.ds((halo + n2 + m) * n1, n1), :] = jnp.where(rows == n1 - 1, 0.0, blk)

        def step(i, c):
            r = pl.multiple_of(i * n1, n1)
            acc = e_ref[pl.ds((halo - lo) * n1 + r, n1), :]
            for dlt in range(-lo + 1, hi + 1):
                acc = acc + e_ref[pl.ds((halo + dlt) * n1 + r, n1), :]
            pos = rows * n2 + i
            cnt = jnp.minimum(pos + (win - lo), seq) - jnp.maximum(pos - lo, 0)
            o_ref[pl.ds(r, n1), :] = acc / cnt.astype(F32) - a_ref[pl.ds(r, n1), :]
            return c

        lax.fori_loop(0, n2, step, 0, unroll=2)

    grp = pl.program_id(1)
    for gi, win in enumerate(POOL_WINDOWS):
        pl.when(grp == gi)(functools.partial(body, win))


def _pool(a, *, n1, n2):
    bsz, seq, width = a.shape
    ngrp = len(POOL_WINDOWS)
    assert width == ngrp * V7X_LANES
    spec = pl.BlockSpec((None, seq, V7X_LANES), lambda b, g: (b, 0, g))
    return pl.pallas_call(
        functools.partial(_pool_kernel, n1=n1, n2=n2),
        grid=(bsz, ngrp),
        in_specs=[spec],
        out_specs=spec,
        out_shape=jax.ShapeDtypeStruct(a.shape, F32),
        scratch_shapes=[pltpu.VMEM(((n2 + 2 * POOL_HALO) * n1, V7X_LANES), F32)],
        compiler_params=_compiler_params(("parallel", "parallel")),
        name="pool",
    )(a)


def _filter_kernel(z_ref, t_ref, w1_ref, b1_ref, q1_ref, w2_ref, b2_ref, q2_ref, wo_ref, dec_ref,
                   f1h_ref, f1l_ref, mh_ref, ml_ref, msh_ref, msl_ref,
                   kf_ref, hid_ref, h_ref, s1_ref, *, n1, n2):
    seq = n1 * n2
    pitch = 2 * n1 + ROW_PAD
    hp = lax.Precision.HIGHEST
    rb = 256

    def mlp(i, c):
        r = pl.multiple_of(i * rb, rb)
        z = z_ref[pl.ds(r, rb), :]
        h = jnp.sin(q1_ref[...] * (jnp.dot(z, w1_ref[...], precision=hp, preferred_element_type=F32) + b1_ref[...]))
        h = jnp.sin(q2_ref[...] * (jnp.dot(h, w2_ref[...], precision=hp, preferred_element_type=F32) + b2_ref[...]))
        hid_ref[pl.ds(r, rb), :] = h
        return c

    @pl.when((pl.program_id(0) == 0) & (pl.program_id(1) == 0))
    def _():
        lax.fori_loop(0, seq // rb, mlp, 0, unroll=4)

    sums = []
    for dr in range(N_DIRS):
        rate = jnp.abs(dec_ref[dr:dr + 1, :])

        def gen(i, acc, dr=dr, rate=rate):
            r = pl.multiple_of(i * rb, rb)
            h = jnp.dot(hid_ref[pl.ds(r, rb), :], wo_ref[dr], precision=hp, preferred_element_type=F32)
            h = h * jnp.exp(-t_ref[pl.ds(r, rb), :] * rate)
            if dr == 1:
                row = lax.broadcasted_iota(jnp.int32, h.shape, 0) + r
                h = jnp.where(row == 0, 0.0, h)
            h_ref[dr, pl.ds(r, rb), :] = h
            return acc + jnp.sum(jnp.abs(h), axis=0, keepdims=True)

        sums.append(lax.fori_loop(0, seq // rb, gen, jnp.zeros((1, h_ref.shape[2]), F32), unroll=4))

    denom = sums[0] + sums[1] + L1_EPS
    for dr in range(N_DIRS):
        def stage1(i, c, dr=dr):
            r = pl.multiple_of(i * n1, n1)
            blk = h_ref[dr, pl.ds(r, n1), :] / denom
            _store_slabs(s1_ref, pl.multiple_of(i * pitch, V7X_SUBLANES), _dot3(f1h_ref[...], f1l_ref[...], blk))
            return c

        lax.fori_loop(0, n2, stage1, 0, unroll=8)

        def emit(row0, x, half, dr=dr):
            if dr == 0:
                kf_ref[pl.ds(row0, 2 * half), :] = x
            else:
                sign = jnp.where(lax.broadcasted_iota(jnp.int32, x.shape, 0) < half, 1.0, -1.0)
                kf_ref[pl.ds(row0, 2 * half), :] += sign * x

        b0 = _gather_pair(s1_ref, 0, n1, n2, pitch)
        x0 = _dot3(msh_ref[...], msl_ref[...], b0)
        emit(0, x0[:2 * n2], n2)
        emit(2 * n2, x0[2 * n2:], n2)

        def stage2(k, c, emit=emit):
            bk = _gather_pair(s1_ref, k, n1 + k, n2, pitch)
            emit(pl.multiple_of((k + 1) * 2 * n2, 2 * n2), _dot3(mh_ref[k], ml_ref[k], bk), n2)
            return c

        lax.fori_loop(1, n1, stage2, 0, unroll=7)


def _filter_spectra(p, tabs, *, n1, n2, width):
    seq = n1 * n2
    cb = 2 * V7X_LANES
    nrow = 2 * n2 * (n1 + 1)
    pitch = 2 * n1 + ROW_PAD
    zfeat, tcol = _positional_features(seq, n1, n2)
    hid = p['w_f2'].shape[0]
    w1 = jnp.zeros((zfeat.shape[1], hid), F32).at[:p['w_f1'].shape[0]].set(p['w_f1'])
    wo = p['w_f_out'].reshape(hid, HYENA_ORDER, N_DIRS, width).transpose(1, 2, 0, 3)
    row = lambda v: v.reshape(1, -1)
    operands = [jnp.asarray(zfeat, F32), jnp.asarray(tcol, F32), w1, row(p['b_f1']), row(p['freq_f1']),
                p['w_f2'], row(p['b_f2']), row(p['freq_f2']), wo,
                p['decay_rate'].reshape(HYENA_ORDER, N_DIRS, width),
                *tabs['f1'], *tabs['m'], *tabs['ms']]
    in_specs = [_const_spec(o.shape) for o in operands]
    in_specs[8] = pl.BlockSpec((None, N_DIRS, hid, cb), lambda c, o: (o, 0, 0, c))
    in_specs[9] = pl.BlockSpec((None, N_DIRS, cb), lambda c, o: (o, 0, c))
    return pl.pallas_call(
        functools.partial(_filter_kernel, n1=n1, n2=n2),
        grid=(width // cb, HYENA_ORDER),
        in_specs=in_specs,
        out_specs=pl.BlockSpec((None, nrow, cb), lambda c, o: (o, 0, c)),
        out_shape=jax.ShapeDtypeStruct((HYENA_ORDER, nrow, width), F32),
        scratch_shapes=[pltpu.VMEM((seq, hid), F32), pltpu.VMEM((N_DIRS, seq, cb), F32),
                        pltpu.VMEM((cb // V7X_LANES, n2 * pitch, V7X_LANES), F32)],
        compiler_params=_compiler_params(("arbitrary", "arbitrary")),
        name="filter_spectra",
    )(*operands)


def _hyena_kernel(x1_ref, x2_ref, v_ref, cw1_ref, cw2_ref, cwv_ref, cb1_ref, cb2_ref, cbv_ref,
                  kf_ref, hb_ref, f1_ref, g_ref, m_ref, mi_ref, ms_ref, mis_ref,
                  z_ref, u_scr, w_scr, s1_ref, s2_ref, *, n1, n2):
    pitch1 = 2 * n1 + ROW_PAD
    pitch2 = 2 * n2 + ROW_PAD
    cbw = z_ref.shape[1]
    rows = lax.broadcasted_iota(jnp.int32, (n1, cbw), 0)

    def short_conv(src, w_ref, b_ref, i):
        r = i * n1
        if i == 0:
            prev = jnp.where(rows == 0, 0.0, src[pl.ds((n2 - 1) * n1 - 1, n1), :])
        else:
            prev = src[pl.ds(r - n1, n1), :]
        if i == n2 - 1:
            nxt = jnp.where(rows == n1 - 1, 0.0, src[pl.ds(1, n1), :])
        else:
            nxt = src[pl.ds(r + n1, n1), :]
        return prev * w_ref[0:1, :] + src[pl.ds(r, n1), :] * w_ref[1:2, :] + nxt * w_ref[2:3, :] + b_ref[...]

    def stage1(i, u):
        _store_slabs(s1_ref, i * pitch1, _dot(f1_ref[...], u.astype(BF16)))

    def middle(o):
        b0 = _gather_pair(s1_ref, 0, n1, n2, pitch1).astype(BF16)
        x0 = _dot(ms_ref[...], b0)
        y0 = jnp.concatenate([_complex_mul(x0[:2 * n2], kf_ref[o, 0:2 * n2, :], n2),
                              _complex_mul(x0[2 * n2:], kf_ref[o, 2 * n2:4 * n2, :], n2)], axis=0)
        _store_slabs(s2_ref, 0, _dot(mis_ref[...], y0.astype(BF16)))
        for k in range(1, n1):
            bk = _gather_pair(s1_ref, k, n1 + k, n2, pitch1).astype(BF16)
            xk = _dot(m_ref[k], bk)
            yk = _complex_mul(xk, kf_ref[o, (k + 1) * 2 * n2:(k + 2) * 2 * n2, :], n2).astype(BF16)
            _store_slabs(s2_ref, k * pitch2, _dot(mi_ref[k], yk))

    def finish(o, gate_ref, gw_ref, gb_ref, u_in, z_out):
        for i in range(n2):
            dk = _gather_pair(s2_ref, i, n2 + i, n1, pitch2).astype(BF16)
            conv = _dot(g_ref[...], dk) + u_in[pl.ds(i * n1, n1), :] * hb_ref[o:o + 1, :]
            z_out[pl.ds(i * n1, n1), :] = short_conv(gate_ref, gw_ref, gb_ref, i) * conv

    for i in range(n2):
        u = short_conv(v_ref, cwv_ref, cbv_ref, i)
        u_scr[pl.ds(i * n1, n1), :] = u
        stage1(i, u)
    middle(0)
    finish(0, x1_ref, cw1_ref, cb1_ref, u_scr, w_scr)
    for i in range(n2):
        stage1(i, w_scr[pl.ds(i * n1, n1), :])
    middle(1)
    finish(1, x2_ref, cw2_ref, cb2_ref, w_scr, z_ref)


def _hyena_block_width(n1, n2):
    seq = n1 * n2
    nrow = 2 * n2 * (n1 + 1)

    def vmem_bytes(cb):
        windows = 2 * 4 * seq * cb
        scratch = 2 * seq * cb + n2 * (2 * n1 + ROW_PAD) * cb + n1 * (2 * n2 + ROW_PAD) * cb
        tables = 2 * (2 * n1 * (2 * n2) ** 2 + 2 * 4 * n2 * 2 * n2 + 2 * 2 * n1 * n1) // 4
        return 4 * (windows + scratch + HYENA_ORDER * nrow * cb + tables)

    return next(c for c in (2 * V7X_LANES, V7X_LANES) if vmem_bytes(c) <= VMEM_LIMIT_BYTES)


def _hyena(u3, conv_w, conv_b, kf, hyena_bias, tabs, *, n1, n2, width):
    cb = _hyena_block_width(n1, n2)
    bsz, seq, _ = u3.shape
    ncb = width // cb
    nrow = kf.shape[1]
    pitch1 = 2 * n1 + ROW_PAD
    pitch2 = 2 * n2 + ROW_PAD
    nslab = cb // V7X_LANES
    seq_spec = lambda part: pl.BlockSpec((None, seq, cb), lambda c, b: (b, 0, part * ncb + c))
    tap_spec = lambda part: pl.BlockSpec((conv_w.shape[0], cb), lambda c, b: (0, part * ncb + c))
    bias_spec = lambda part: pl.BlockSpec((1, cb), lambda c, b: (0, part * ncb + c))
    consts = [tabs['f1'][0], tabs['g'][0], tabs['m'][0], tabs['mi'][0], tabs['ms'][0], tabs['mis'][0]]
    return pl.pallas_call(
        functools.partial(_hyena_kernel, n1=n1, n2=n2),
        grid=(ncb, bsz),
        in_specs=[seq_spec(0), seq_spec(1), seq_spec(2),
                  tap_spec(0), tap_spec(1), tap_spec(2), bias_spec(0), bias_spec(1), bias_spec(2),
                  pl.BlockSpec((HYENA_ORDER, nrow, cb), lambda c, b: (0, 0, c), pipeline_mode=pl.Buffered(1)),
                  pl.BlockSpec((HYENA_ORDER, cb), lambda c, b: (0, c)),
                  *[_const_spec(t.shape) for t in consts]],
        out_specs=pl.BlockSpec((None, seq, cb), lambda c, b: (b, 0, c)),
        out_shape=jax.ShapeDtypeStruct((bsz, seq, width), F32),
        scratch_shapes=[pltpu.VMEM((seq, cb), F32), pltpu.VMEM((seq, cb), F32),
                        pltpu.VMEM((nslab, n2 * pitch1, V7X_LANES), F32),
                        pltpu.VMEM((nslab, n1 * pitch2, V7X_LANES), F32)],
        compiler_params=_compiler_params(("parallel", "parallel")),
        name="hyena",
    )(u3, u3, u3, conv_w, conv_w, conv_w, conv_b, conv_b, conv_b, kf, hyena_bias, *consts)


def _merge_ffn_kernel(x_ref, permt_ref, pm_ref, z_ref, gate_ref, lng_ref, lnb_ref, wp_ref, bp_ref, ps_ref, wpp_ref,
                      whp_ref, wo_ref, bo_ref, g1_ref, b1_ref, w1_ref, bf1_ref, w2_ref, bf2_ref, g2_ref, b2_ref,
                      y_ref, *, ffn_chunk):
    tm, d = z_ref.shape
    ngrp = wp_ref.shape[0]
    gd = wp_ref.shape[1]
    pm = pm_ref[...].astype(BF16)
    p = jnp.concatenate([_dot(pm[:, g * gd:(g + 1) * gd], wp_ref[g]) for g in range(ngrp)], axis=1)
    p = (p + bp_ref[...]) * ps_ref[...]
    ya = _dot(p.astype(BF16), wpp_ref[...])
    yb = _dot(z_ref[...].astype(BF16), whp_ref[...])
    m = gate_ref[:, :d] * ya + gate_ref[:, d:] * yb
    m = _dot(permt_ref[...], m.astype(BF16)).astype(BF16)
    mo = _dot(m, wo_ref[...]) + bo_ref[...]
    h0 = _layer_norm(x_ref[...].reshape(tm, d), lng_ref[...], lnb_ref[...])
    h = _layer_norm(DN_ALPHA * h0 + mo, g1_ref[...], b1_ref[...])
    hb = h.astype(BF16)
    acc = jnp.zeros(h.shape, F32)
    for c in range(0, w1_ref.shape[1], ffn_chunk):
        u = jnp.maximum(_dot(hb, w1_ref[:, c:c + ffn_chunk]) + bf1_ref[:, c:c + ffn_chunk], 0.0)
        acc = acc + _dot((u * u).astype(BF16), w2_ref[c:c + ffn_chunk, :])
    y = _layer_norm(DN_ALPHA * h + acc + bf2_ref[...], g2_ref[...], b2_ref[...])
    y_ref[...] = y.reshape(y_ref.shape)


def _merge_ffn(xv, permt, pm, z, gate, consts, *, n1, n2):
    bsz, seq, d = z.shape
    j_per = TOKEN_TILE // n1
    tok = lambda w: pl.BlockSpec((None, TOKEN_TILE, w), lambda b, t: (b, t, 0))
    nat = pl.BlockSpec((None, n1, j_per, d), lambda b, t: (b, 0, t, 0))
    return pl.pallas_call(
        functools.partial(_merge_ffn_kernel, ffn_chunk=1024),
        grid=(bsz, n2 // j_per),
        in_specs=[nat, _const_spec(permt.shape), tok(pm.shape[2]), tok(d), tok(gate.shape[2]),
                  *[_const_spec(c.shape) for c in consts]],
        out_specs=nat,
        out_shape=jax.ShapeDtypeStruct(xv.shape, F32),
        compiler_params=_compiler_params(("parallel", "parallel")),
        name="merge_ffn",
    )(xv, permt, pm, z, gate, *consts)


def _encoder(x, p):
    bsz, seq, d = x.shape
    n1, n2 = _split_factors(seq)
    pool_w = p['w_pool_proj'].shape[0]
    hy_w = p['w_hyena_proj'].shape[0]
    row = lambda v: v.reshape(1, -1).astype(F32)
    tabs = {k: _hi_lo(v) for k, v in _dft_tables(n1, n2).items()}

    xv = x.reshape(bsz, n1, n2, d)
    perm = _row_permutation(n1, TOKEN_TILE // n1)
    a, u3, gate = _inproj(xv, jnp.asarray(perm, BF16), row(p['ln_in_g']), row(p['ln_in_b']),
                          p['w_in'].astype(BF16), row(p['b_in']),
                          n1=n1, n2=n2, widths=(pool_w, HYENA_ORDER * hy_w + hy_w, 2 * d))
    pm = _pool(a, n1=n1, n2=n2)
    kf = _filter_spectra(p, tabs, n1=n1, n2=n2, width=hy_w)
    z = _hyena(u3, p['conv_w'], row(p['conv_b']), kf, p['hyena_bias'], tabs, n1=n1, n2=n2, width=hy_w)
    merge_consts = [row(p['ln_in_g']), row(p['ln_in_b']), p['w_pool'].astype(BF16), row(p['b_pool']),
                    row(p['pool_scale']), p['w_pool_proj'].astype(BF16), p['w_hyena_proj'].astype(BF16),
                    p['w_o'].astype(BF16), row(p['b_o']), row(p['ln1_g']), row(p['ln1_b']),
                    p['w_ff1'].astype(BF16), row(p['b_ff1']), p['w_ff2'].astype(BF16), row(p['b_ff2']),
                    row(p['ln2_g']), row(p['ln2_b'])]
    y = _merge_ffn(xv, jnp.asarray(perm.T, BF16), pm, z, gate, merge_consts, n1=n1, n2=n2)
    return y.reshape(bsz, seq, d)


def kernel(x_prompt, x_sample, ln_in_g, ln_in_b, w_in, b_in, w_pool, b_pool, pool_scale, w_pool_proj, conv_w, conv_b, w_f1, b_f1, freq_f1, w_f2, b_f2, freq_f2, w_f_out, decay_rate, hyena_bias, w_hyena_proj, w_o, b_o, ln1_g, ln1_b, w_ff1, b_ff1, w_ff2, b_ff2, ln2_g, ln2_b):
    layer = dict(w_in=w_in, b_in=b_in, w_pool=w_pool, b_pool=b_pool, pool_scale=pool_scale,
                 w_pool_proj=w_pool_proj, conv_w=conv_w, conv_b=conv_b, w_f1=w_f1, b_f1=b_f1, freq_f1=freq_f1,
                 w_f2=w_f2, b_f2=b_f2, freq_f2=freq_f2, w_f_out=w_f_out, decay_rate=decay_rate,
                 hyena_bias=hyena_bias, w_hyena_proj=w_hyena_proj, w_o=w_o, b_o=b_o, ln1_g=ln1_g, ln1_b=ln1_b,
                 w_ff1=w_ff1, b_ff1=b_ff1, w_ff2=w_ff2, b_ff2=b_ff2, ln2_g=ln2_g, ln2_b=ln2_b)
    assert all(v.shape[0] == DEPTH for v in layer.values())
    p = {k: v[0] for k, v in layer.items()}
    p.update(ln_in_g=ln_in_g, ln_in_b=ln_in_b)
    y_prompt = _encoder(x_prompt, p)
    y_sample = _encoder(x_sample, p)
    return (y_prompt, y_sample)
```

```python
import functools
import math

import numpy as np
import jax
import jax.numpy as jnp
from jax import lax
from jax.experimental import pallas as pl
from jax.experimental.pallas import tpu as pltpu

F32 = jnp.float32
BF16 = jnp.bfloat16

V7X_LANES = 128
V7X_SUBLANES = 8
V7X_VMEM_BYTES = 64 * 1024 * 1024
VMEM_LIMIT_BYTES = V7X_VMEM_BYTES - 8 * 1024 * 1024

POOL_WINDOWS = (2, 4, 8, 16)
POOL_HALO = max(max(w // 2, w - 1 - w // 2) for w in POOL_WINDOWS)
N_DIRS = 2
HYENA_ORDER = 2
POS_BANDS = 16
LN_EPS = 1e-5
L1_EPS = 1e-6
DEPTH = 1
DN_ALPHA = (2.0 * DEPTH) ** 0.25

ROW_PAD = V7X_SUBLANES
TOKEN_TILE = 512


def _compiler_params(semantics):
    return pltpu.CompilerParams(dimension_semantics=semantics, vmem_limit_bytes=VMEM_LIMIT_BYTES)


def _const_spec(shape):
    nd = len(shape)
    return pl.BlockSpec(shape, lambda *_: (0,) * nd, pipeline_mode=pl.Buffered(1))


def _split_factors(seq_len):
    n2 = 64
    assert seq_len % n2 == 0
    return seq_len // n2, n2


def _dft_tables(n1, n2):
    seq = n1 * n2
    i1 = np.arange(n1)
    th = np.pi * np.outer(i1, i1) / n1
    f1 = np.zeros((2 * n1, n1))
    f1[:n1] = np.cos(th)
    f1[n1] = (-1.0) ** i1
    f1[n1 + 1:] = -np.sin(th[1:])
    g = np.zeros((n1, 2 * n1))
    g[:, :n1] = np.cos(th.T)
    g[:, n1] = (-1.0) ** i1
    g[:, n1 + 1:] = -np.sin(th.T[:, 1:])
    i2 = np.arange(n2)
    base = 2 * np.pi * np.outer(i2, i2) / n2
    m = np.zeros((n1, 2 * n2, 2 * n2))
    mi = np.zeros((n1, 2 * n2, 2 * n2))
    for k in range(1, n1):
        phi = base + 2 * np.pi * k * i2[None, :] / (2 * seq)
        c, s = np.cos(phi), np.sin(phi)
        m[k] = np.block([[c, s], [-s, c]])
        mi[k] = np.block([[c.T, -s.T], [s.T, c.T]]) / seq
    phin = base + np.pi * i2[None, :] / n2
    z = np.zeros((n2, n2))
    ms = np.block([[np.cos(base), z], [-np.sin(base), z], [z, np.cos(phin)], [z, -np.sin(phin)]])
    mis = np.block([[np.cos(base).T, -np.sin(base).T, z, z],
                    [z, z, np.cos(phin).T, -np.sin(phin).T]]) / (2 * seq)
    return dict(f1=f1, g=g, m=m, mi=mi, ms=ms, mis=mis)


def _perm_positions(n1, n2):
    return (np.arange(n2)[:, None] + n2 * np.arange(n1)[None, :]).reshape(-1)


def _positional_features(seq_len, n1, n2):
    pos = _perm_positions(n1, n2).astype(np.float64)
    t = pos / (seq_len - 1)
    w = 2.0 * math.pi * pos / seq_len
    f = np.linspace(1e-4, POS_BANDS - 1, POS_BANDS)
    z = np.concatenate([t[:, None], np.cos(f[None] * w[:, None]), -np.sin(f[None] * w[:, None])], axis=-1)
    zp = np.zeros((seq_len, 40))
    zp[:, :z.shape[1]] = z
    return zp, t[:, None]


def _layer_norm(x, g, b):
    mu = jnp.mean(x, axis=-1, keepdims=True)
    xc = x - mu
    var = jnp.mean(xc * xc, axis=-1, keepdims=True)
    return xc * lax.rsqrt(var + LN_EPS) * g + b


def _dot(a, b):
    return jnp.dot(a, b, preferred_element_type=F32)


def _split_bf16(x):
    hi = x.astype(BF16)
    return hi, (x - hi.astype(F32)).astype(BF16)


def _dot3(a, b):
    a_hi, a_lo = _split_bf16(a)
    b_hi, b_lo = _split_bf16(b)
    return _dot(a_hi, b_hi) + (_dot(a_hi, b_lo) + _dot(a_lo, b_hi))


def _store_slabs(dst_ref, row0, val):
    rows = val.shape[0]
    for j in range(dst_ref.shape[0]):
        dst_ref[j, pl.ds(row0, rows), :] = val[:, j * V7X_LANES:(j + 1) * V7X_LANES]


def _gather_pair(src_ref, start_a, start_b, count, pitch):
    cols = []
    for j in range(src_ref.shape[0]):
        a = src_ref[j, pl.ds(start_a, count, stride=pitch), :]
        b = src_ref[j, pl.ds(start_b, count, stride=pitch), :]
        cols.append(jnp.concatenate([a, b], axis=0))
    return cols[0] if len(cols) == 1 else jnp.concatenate(cols, axis=1)


def _complex_mul(x, k, half):
    xr, xi = x[:half], x[half:]
    kr, ki = k[:half], k[half:]
    return jnp.concatenate([xr * kr - xi * ki, xr * ki + xi * kr], axis=0)


def _row_permutation(n1, j_per):
    i, j = np.meshgrid(np.arange(n1), np.arange(j_per), indexing="ij")
    p = np.zeros((n1 * j_per, n1 * j_per), np.float32)
    p[(j * n1 + i).ravel(), (i * j_per + j).ravel()] = 1.0
    return p


def _inproj_kernel(x_ref, perm_ref, g_ref, b_ref, w_ref, bias_ref, a_ref, u_ref, gate_ref, h_scr, *, chunk):
    n1, j_per, d = x_ref.shape
    for i in range(0, n1, V7X_SUBLANES):
        x = x_ref[i:i + V7X_SUBLANES].reshape(V7X_SUBLANES * j_per, d)
        h_scr[i * j_per:(i + V7X_SUBLANES) * j_per, :] = _layer_norm(x, g_ref[...], b_ref[...]).astype(BF16)
    h = _dot(perm_ref[...], h_scr[...]).astype(BF16)
    wa, wu = a_ref.shape[1], u_ref.shape[1]
    total = w_ref.shape[1]
    for c in range(0, total, chunk):
        y = _dot(h, w_ref[:, c:c + chunk]) + bias_ref[:, c:c + chunk]
        if c < wa:
            a_ref[:, c:c + chunk] = y
        elif c < wa + wu:
            u_ref[:, c - wa:c - wa + chunk] = y
        else:
            gate_ref[:, c - wa - wu:c - wa - wu + chunk] = jax.nn.sigmoid(y)


def _inproj(xv, perm, ln_g, ln_b, w_in, b_in, *, n1, n2, widths):
    bsz = xv.shape[0]
    d = w_in.shape[0]
    seq = n1 * n2
    j_per = TOKEN_TILE // n1
    wa, wu, wg = widths
    chunk = 512
    assert wa % chunk == 0 and wu % chunk == 0 and wg % chunk == 0 and n2 % j_per == 0
    out_shapes = tuple(jax.ShapeDtypeStruct((bsz, seq, w), F32) for w in widths)
    tok = lambda w: pl.BlockSpec((None, TOKEN_TILE, w), lambda b, t: (b, t, 0))
    return pl.pallas_call(
        functools.partial(_inproj_kernel, chunk=chunk),
        grid=(bsz, n2 // j_per),
        in_specs=[
            pl.BlockSpec((None, n1, j_per, d), lambda b, t: (b, 0, t, 0)),
            _const_spec(perm.shape), _const_spec((1, d)), _const_spec((1, d)),
            _const_spec(w_in.shape), _const_spec(b_in.shape),
        ],
        out_specs=[tok(wa), tok(wu), tok(wg)],
        out_shape=out_shapes,
        scratch_shapes=[pltpu.VMEM((TOKEN_TILE, d), BF16)],
        compiler_params=_compiler_params(("parallel", "parallel")),
        name="inproj",
    )(xv, perm, ln_g, ln_b, w_in, b_in)


def _pool_kernel(a_ref, o_ref, e_ref, *, n1, n2):
    seq = n1 * n2
    halo = POOL_HALO
    rows = lax.broadcasted_iota(jnp.int32, (n1, V7X_LANES), 0)

    def body(win):
        lo = win // 2
        hi = win - 1 - lo

        def copy(i, c):
            r = pl.multiple_of(i * n1, n1)
            e_ref[pl.ds(halo * n1 + r, n1), :] = a_ref[pl.ds(r, n1), :]
            return c

        lax.fori_loop(0, n2, copy, 0, unroll=4)
        for m in range(1, lo + 1):
            blk = a_ref[pl.ds((n2 - m) * n1 - 1, n1), :]
            e_ref[pl.ds((halo - m) * n1, n1), :] = jnp.where(rows == 0, 0.0, blk)
        for m in range(hi):
            blk = a_ref[pl.ds(m * n1 + 1, n1), :]
            e_ref[pl.ds((halo + n2 + m) * n1, n1), :] = jnp.where(rows == n1 - 1, 0.0, blk)

        def step(i, c):
            r = pl.multiple_of(i * n1, n1)
            acc = e_ref[pl.ds((halo - lo) * n1 + r, n1), :]
            for dlt in range(-lo + 1, hi + 1):
                acc = acc + e_ref[pl.ds((halo + dlt) * n1 + r, n1), :]
            pos = rows * n2 + i
            cnt = jnp.minimum(pos + (win - lo), seq) - jnp.maximum(pos - lo, 0)
            o_ref[pl.ds(r, n1), :] = acc / cnt.astype(F32) - a_ref[pl.ds(r, n1), :]
            return c

        lax.fori_loop(0, n2, step, 0, unroll=2)

    grp = pl.program_id(1)
    for gi, win in enumerate(POOL_WINDOWS):
        pl.when(grp == gi)(functools.partial(body, win))


def _pool(a, *, n1, n2):
    bsz, seq, width = a.shape
    ngrp = len(POOL_WINDOWS)
    assert width == ngrp * V7X_LANES
    spec = pl.BlockSpec((None, seq, V7X_LANES), lambda b, g: (b, 0, g))
    return pl.pallas_call(
        functools.partial(_pool_kernel, n1=n1, n2=n2),
        grid=(bsz, ngrp),
        in_specs=[spec],
        out_specs=spec,
        out_shape=jax.ShapeDtypeStruct(a.shape, F32),
        scratch_shapes=[pltpu.VMEM(((n2 + 2 * POOL_HALO) * n1, V7X_LANES), F32)],
        compiler_params=_compiler_params(("parallel", "parallel")),
        name="pool",
    )(a)


def _filter_kernel(z_ref, t_ref, w1_ref, b1_ref, q1_ref, w2_ref, b2_ref, q2_ref, wo_ref, dec_ref,
                   f1_ref, m_ref, ms_ref, kf_ref, hid_ref, h_ref, s1_ref, *, n1, n2):
    seq = n1 * n2
    pitch = 2 * n1 + ROW_PAD
    hp = lax.Precision.HIGHEST
    rb = 256

    def mlp(i, c):
        r = pl.multiple_of(i * rb, rb)
        z = z_ref[pl.ds(r, rb), :]
        h = jnp.sin(q1_ref[...] * (jnp.dot(z, w1_ref[...], precision=hp, preferred_element_type=F32) + b1_ref[...]))
        h = jnp.sin(q2_ref[...] * (jnp.dot(h, w2_ref[...], precision=hp, preferred_element_type=F32) + b2_ref[...]))
        hid_ref[pl.ds(r, rb), :] = h
        return c

    @pl.when((pl.program_id(0) == 0) & (pl.program_id(1) == 0))
    def _():
        lax.fori_loop(0, seq // rb, mlp, 0, unroll=4)

    sums = []
    for dr in range(N_DIRS):
        rate = jnp.abs(dec_ref[dr:dr + 1, :])

        def gen(i, acc, dr=dr, rate=rate):
            r = pl.multiple_of(i * rb, rb)
            h = _dot3(hid_ref[pl.ds(r, rb), :], wo_ref[dr])
            h = h * jnp.exp(-t_ref[pl.ds(r, rb), :] * rate)
            if dr == 1:
                row = lax.broadcasted_iota(jnp.int32, h.shape, 0) + r
                h = jnp.where(row == 0, 0.0, h)
            h_ref[dr, pl.ds(r, rb), :] = h
            return acc + jnp.sum(jnp.abs(h), axis=0, keepdims=True)

        sums.append(lax.fori_loop(0, seq // rb, gen, jnp.zeros((1, h_ref.shape[2]), F32), unroll=4))

    denom = sums[0] + sums[1] + L1_EPS
    for dr in range(N_DIRS):
        def stage1(i, c, dr=dr):
            r = pl.multiple_of(i * n1, n1)
            blk = h_ref[dr, pl.ds(r, n1), :] / denom
            _store_slabs(s1_ref, pl.multiple_of(i * pitch, V7X_SUBLANES), _dot(f1_ref[...], blk.astype(BF16)))
            return c

        lax.fori_loop(0, n2, stage1, 0, unroll=8)

        def emit(row0, x, half, dr=dr):
            if dr == 0:
                kf_ref[pl.ds(row0, 2 * half), :] = x
            else:
                sign = jnp.where(lax.broadcasted_iota(jnp.int32, x.shape, 0) < half, 1.0, -1.0)
                kf_ref[pl.ds(row0, 2 * half), :] += sign * x

        b0 = _gather_pair(s1_ref, 0, n1, n2, pitch).astype(BF16)
        x0 = _dot(ms_ref[...], b0)
        emit(0, x0[:2 * n2], n2)
        emit(2 * n2, x0[2 * n2:], n2)

        def stage2(k, c, emit=emit):
            bk = _gather_pair(s1_ref, k, n1 + k, n2, pitch).astype(BF16)
            emit(pl.multiple_of((k + 1) * 2 * n2, 2 * n2), _dot(m_ref[k], bk), n2)
            return c

        lax.fori_loop(1, n1, stage2, 0, unroll=7)


def _filter_spectra(p, tabs, *, n1, n2, width):
    seq = n1 * n2
    cb = 2 * V7X_LANES
    nrow = 2 * n2 * (n1 + 1)
    pitch = 2 * n1 + ROW_PAD
    zfeat, tcol = _positional_features(seq, n1, n2)
    hid = p['w_f2'].shape[0]
    w1 = jnp.zeros((zfeat.shape[1], hid), F32).at[:p['w_f1'].shape[0]].set(p['w_f1'])
    wo = p['w_f_out'].reshape(hid, HYENA_ORDER, N_DIRS, width).transpose(1, 2, 0, 3)
    row = lambda v: v.reshape(1, -1)
    operands = [jnp.asarray(zfeat, F32), jnp.asarray(tcol, F32), w1, row(p['b_f1']), row(p['freq_f1']),
                p['w_f2'], row(p['b_f2']), row(p['freq_f2']), wo,
                p['decay_rate'].reshape(HYENA_ORDER, N_DIRS, width),
                tabs['f1'], tabs['m'], tabs['ms']]
    in_specs = [_const_spec(o.shape) for o in operands]
    in_specs[8] = pl.BlockSpec((None, N_DIRS, hid, cb), lambda c, o: (o, 0, 0, c))
    in_specs[9] = pl.BlockSpec((None, N_DIRS, cb), lambda c, o: (o, 0, c))
    return pl.pallas_call(
        functools.partial(_filter_kernel, n1=n1, n2=n2),
        grid=(width // cb, HYENA_ORDER),
        in_specs=in_specs,
        out_specs=pl.BlockSpec((None, nrow, cb), lambda c, o: (o, 0, c)),
        out_shape=jax.ShapeDtypeStruct((HYENA_ORDER, nrow, width), F32),
        scratch_shapes=[pltpu.VMEM((seq, hid), F32), pltpu.VMEM((N_DIRS, seq, cb), F32),
                        pltpu.VMEM((cb // V7X_LANES, n2 * pitch, V7X_LANES), F32)],
        compiler_params=_compiler_params(("arbitrary", "arbitrary")),
        name="filter_spectra",
    )(*operands)


def _hyena_kernel(x1_ref, x2_ref, v_ref, cw1_ref, cw2_ref, cwv_ref, cb1_ref, cb2_ref, cbv_ref,
                  kf_ref, hb_ref, f1_ref, g_ref, m_ref, mi_ref, ms_ref, mis_ref,
                  z_ref, u_scr, w_scr, s1_ref, s2_ref, *, n1, n2):
    pitch1 = 2 * n1 + ROW_PAD
    pitch2 = 2 * n2 + ROW_PAD
    cbw = z_ref.shape[1]
    rows = lax.broadcasted_iota(jnp.int32, (n1, cbw), 0)

    def short_conv(src, w_ref, b_ref, i):
        r = i * n1
        if i == 0:
            prev = jnp.where(rows == 0, 0.0, src[pl.ds((n2 - 1) * n1 - 1, n1), :])
        else:
            prev = src[pl.ds(r - n1, n1), :]
        if i == n2 - 1:
            nxt = jnp.where(rows == n1 - 1, 0.0, src[pl.ds(1, n1), :])
        else:
            nxt = src[pl.ds(r + n1, n1), :]
        return prev * w_ref[0:1, :] + src[pl.ds(r, n1), :] * w_ref[1:2, :] + nxt * w_ref[2:3, :] + b_ref[...]

    def stage1(i, u):
        _store_slabs(s1_ref, i * pitch1, _dot(f1_ref[...], u.astype(BF16)))

    def middle(o):
        b0 = _gather_pair(s1_ref, 0, n1, n2, pitch1).astype(BF16)
        x0 = _dot(ms_ref[...], b0)
        y0 = jnp.concatenate([_complex_mul(x0[:2 * n2], kf_ref[o, 0:2 * n2, :], n2),
                              _complex_mul(x0[2 * n2:], kf_ref[o, 2 * n2:4 * n2, :], n2)], axis=0)
        _store_slabs(s2_ref, 0, _dot(mis_ref[...], y0.astype(BF16)))
        for k in range(1, n1):
            bk = _gather_pair(s1_ref, k, n1 + k, n2, pitch1).astype(BF16)
            xk = _dot(m_ref[k], bk)
            yk = _complex_mul(xk, kf_ref[o, (k + 1) * 2 * n2:(k + 2) * 2 * n2, :], n2).astype(BF16)
            _store_slabs(s2_ref, k * pitch2, _dot(mi_ref[k], yk))

    def finish(o, gate_ref, gw_ref, gb_ref, u_in, z_out):
        for i in range(n2):
            dk = _gather_pair(s2_ref, i, n2 + i, n1, pitch2).astype(BF16)
            conv = _dot(g_ref[...], dk) + u_in[pl.ds(i * n1, n1), :] * hb_ref[o:o + 1, :]
            z_out[pl.ds(i * n1, n1), :] = short_conv(gate_ref, gw_ref, gb_ref, i) * conv

    for i in range(n2):
        u = short_conv(v_ref, cwv_ref, cbv_ref, i)
        u_scr[pl.ds(i * n1, n1), :] = u
        stage1(i, u)
    middle(0)
    finish(0, x1_ref, cw1_ref, cb1_ref, u_scr, w_scr)
    for i in range(n2):
        stage1(i, w_scr[pl.ds(i * n1, n1), :])
    middle(1)
    finish(1, x2_ref, cw2_ref, cb2_ref, w_scr, z_ref)


def _hyena_block_width(n1, n2):
    seq = n1 * n2
    nrow = 2 * n2 * (n1 + 1)

    def vmem_bytes(cb):
        windows = 2 * 4 * seq * cb
        scratch = 2 * seq * cb + n2 * (2 * n1 + ROW_PAD) * cb + n1 * (2 * n2 + ROW_PAD) * cb
        tables = 2 * (2 * n1 * (2 * n2) ** 2 + 2 * 4 * n2 * 2 * n2 + 2 * 2 * n1 * n1) // 4
        return 4 * (windows + scratch + HYENA_ORDER * nrow * cb + tables)

    return next(c for c in (2 * V7X_LANES, V7X_LANES) if vmem_bytes(c) <= VMEM_LIMIT_BYTES)


def _hyena(u3, conv_w, conv_b, kf, hyena_bias, tabs, *, n1, n2, width):
    cb = _hyena_block_width(n1, n2)
    bsz, seq, _ = u3.shape
    ncb = width // cb
    nrow = kf.shape[1]
    pitch1 = 2 * n1 + ROW_PAD
    pitch2 = 2 * n2 + ROW_PAD
    nslab = cb // V7X_LANES
    seq_spec = lambda part: pl.BlockSpec((None, seq, cb), lambda c, b: (b, 0, part * ncb + c))
    tap_spec = lambda part: pl.BlockSpec((conv_w.shape[0], cb), lambda c, b: (0, part * ncb + c))
    bias_spec = lambda part: pl.BlockSpec((1, cb), lambda c, b: (0, part * ncb + c))
    consts = [tabs[k] for k in ('f1', 'g', 'm', 'mi', 'ms', 'mis')]
    return pl.pallas_call(
        functools.partial(_hyena_kernel, n1=n1, n2=n2),
        grid=(ncb, bsz),
        in_specs=[seq_spec(0), seq_spec(1), seq_spec(2),
                  tap_spec(0), tap_spec(1), tap_spec(2), bias_spec(0), bias_spec(1), bias_spec(2),
                  pl.BlockSpec((HYENA_ORDER, nrow, cb), lambda c, b: (0, 0, c), pipeline_mode=pl.Buffered(1)),
                  pl.BlockSpec((HYENA_ORDER, cb), lambda c, b: (0, c)),
                  *[_const_spec(t.shape) for t in consts]],
        out_specs=pl.BlockSpec((None, seq, cb), lambda c, b: (b, 0, c)),
        out_shape=jax.ShapeDtypeStruct((bsz, seq, width), F32),
        scratch_shapes=[pltpu.VMEM((seq, cb), F32), pltpu.VMEM((seq, cb), F32),
                        pltpu.VMEM((nslab, n2 * pitch1, V7X_LANES), F32),
                        pltpu.VMEM((nslab, n1 * pitch2, V7X_LANES), F32)],
        compiler_params=_compiler_params(("parallel", "parallel")),
        name="hyena",
    )(u3, u3, u3, conv_w, conv_w, conv_w, conv_b, conv_b, conv_b, kf, hyena_bias, *consts)


def _merge_ffn_kernel(x_ref, permt_ref, pm_ref, z_ref, gate_ref, lng_ref, lnb_ref, wp_ref, bp_ref, ps_ref, wpp_ref,
                      whp_ref, wo_ref, bo_ref, g1_ref, b1_ref, w1_ref, bf1_ref, w2_ref, bf2_ref, g2_ref, b2_ref,
                      y_ref, *, ffn_chunk):
    tm, d = z_ref.shape
    ngrp = wp_ref.shape[0]
    gd = wp_ref.shape[1]
    pm = pm_ref[...].astype(BF16)
    p = jnp.concatenate([_dot(pm[:, g * gd:(g + 1) * gd], wp_ref[g]) for g in range(ngrp)], axis=1)
    p = (p + bp_ref[...]) * ps_ref[...]
    ya = _dot(p.astype(BF16), wpp_ref[...])
    yb = _dot(z_ref[...].astype(BF16), whp_ref[...])
    m = gate_ref[:, :d] * ya + gate_ref[:, d:] * yb
    m = _dot(permt_ref[...], m.astype(BF16)).astype(BF16)
    mo = _dot(m, wo_ref[...]) + bo_ref[...]
    h0 = _layer_norm(x_ref[...].reshape(tm, d), lng_ref[...], lnb_ref[...])
    h = _layer_norm(DN_ALPHA * h0 + mo, g1_ref[...], b1_ref[...])
    hb = h.astype(BF16)
    acc = jnp.zeros(h.shape, F32)
    for c in range(0, w1_ref.shape[1], ffn_chunk):
        u = jnp.maximum(_dot(hb, w1_ref[:, c:c + ffn_chunk]) + bf1_ref[:, c:c + ffn_chunk], 0.0)
        acc = acc + _dot((u * u).astype(BF16), w2_ref[c:c + ffn_chunk, :])
    y = _layer_norm(DN_ALPHA * h + acc + bf2_ref[...], g2_ref[...], b2_ref[...])
    y_ref[...] = y.reshape(y_ref.shape)


def _merge_ffn(xv, permt, pm, z, gate, consts, *, n1, n2):
    bsz, seq, d = z.shape
    j_per = TOKEN_TILE // n1
    tok = lambda w: pl.BlockSpec((None, TOKEN_TILE, w), lambda b, t: (b, t, 0))
    nat = pl.BlockSpec((None, n1, j_per, d), lambda b, t: (b, 0, t, 0))
    return pl.pallas_call(
        functools.partial(_merge_ffn_kernel, ffn_chunk=1024),
        grid=(bsz, n2 // j_per),
        in_specs=[nat, _const_spec(permt.shape), tok(pm.shape[2]), tok(d), tok(gate.shape[2]),
                  *[_const_spec(c.shape) for c in consts]],
        out_specs=nat,
        out_shape=jax.ShapeDtypeStruct(xv.shape, F32),
        compiler_params=_compiler_params(("parallel", "parallel")),
        name="merge_ffn",
    )(xv, permt, pm, z, gate, *consts)


def _encoder(x, p):
    bsz, seq, d = x.shape
    n1, n2 = _split_factors(seq)
    pool_w = p['w_pool_proj'].shape[0]
    hy_w = p['w_hyena_proj'].shape[0]
    row = lambda v: v.reshape(1, -1).astype(F32)
    tabs = {k: jnp.asarray(v, BF16) for k, v in _dft_tables(n1, n2).items()}

    xv = x.reshape(bsz, n1, n2, d)
    perm = _row_permutation(n1, TOKEN_TILE // n1)
    a, u3, gate = _inproj(xv, jnp.asarray(perm, BF16), row(p['ln_in_g']), row(p['ln_in_b']),
                          p['w_in'].astype(BF16), row(p['b_in']),
                          n1=n1, n2=n2, widths=(pool_w, HYENA_ORDER * hy_w + hy_w, 2 * d))
    pm = _pool(a, n1=n1, n2=n2)
    kf = _filter_spectra(p, tabs, n1=n1, n2=n2, width=hy_w)
    z = _hyena(u3, p['conv_w'], row(p['conv_b']), kf, p['hyena_bias'], tabs, n1=n1, n2=n2, width=hy_w)
    merge_consts = [row(p['ln_in_g']), row(p['ln_in_b']), p['w_pool'].astype(BF16), row(p['b_pool']),
                    row(p['pool_scale']), p['w_pool_proj'].astype(BF16), p['w_hyena_proj'].astype(BF16),
                    p['w_o'].astype(BF16), row(p['b_o']), row(p['ln1_g']), row(p['ln1_b']),
                    p['w_ff1'].astype(BF16), row(p['b_ff1']), p['w_ff2'].astype(BF16), row(p['b_ff2']),
                    row(p['ln2_g']), row(p['ln2_b'])]
    y = _merge_ffn(xv, jnp.asarray(perm.T, BF16), pm, z, gate, merge_consts, n1=n1, n2=n2)
    return y.reshape(bsz, seq, d)


def kernel(x_prompt, x_sample, ln_in_g, ln_in_b, w_in, b_in, w_pool, b_pool, pool_scale, w_pool_proj, conv_w, conv_b, w_f1, b_f1, freq_f1, w_f2, b_f2, freq_f2, w_f_out, decay_rate, hyena_bias, w_hyena_proj, w_o, b_o, ln1_g, ln1_b, w_ff1, b_ff1, w_ff2, b_ff2, ln2_g, ln2_b):
    layer = dict(w_in=w_in, b_in=b_in, w_pool=w_pool, b_pool=b_pool, pool_scale=pool_scale,
                 w_pool_proj=w_pool_proj, conv_w=conv_w, conv_b=conv_b, w_f1=w_f1, b_f1=b_f1, freq_f1=freq_f1,
                 w_f2=w_f2, b_f2=b_f2, freq_f2=freq_f2, w_f_out=w_f_out, decay_rate=decay_rate,
                 hyena_bias=hyena_bias, w_hyena_proj=w_hyena_proj, w_o=w_o, b_o=b_o, ln1_g=ln1_g, ln1_b=ln1_b,
                 w_ff1=w_ff1, b_ff1=b_ff1, w_ff2=w_ff2, b_ff2=b_ff2, ln2_g=ln2_g, ln2_b=ln2_b)
    assert all(v.shape[0] == DEPTH for v in layer.values())
    p = {k: v[0] for k, v in layer.items()}
    p.update(ln_in_g=ln_in_g, ln_in_b=ln_in_b)
    y_prompt = _encoder(x_prompt, p)
    y_sample = _encoder(x_sample, p)
    return (y_prompt, y_sample)
```

```python
import functools
import math

import numpy as np
import jax
import jax.numpy as jnp
from jax import lax
from jax.experimental import pallas as pl
from jax.experimental.pallas import tpu as pltpu

F32 = jnp.float32
BF16 = jnp.bfloat16

V7X_LANES = 128
V7X_SUBLANES = 8
V7X_VMEM_BYTES = 64 * 1024 * 1024
VMEM_LIMIT_BYTES = V7X_VMEM_BYTES - 8 * 1024 * 1024

POOL_WINDOWS = (2, 4, 8, 16)
POOL_HALO = max(max(w // 2, w - 1 - w // 2) for w in POOL_WINDOWS)
N_DIRS = 2
HYENA_ORDER = 2
POS_BANDS = 16
LN_EPS = 1e-5
L1_EPS = 1e-6
DEPTH = 1
DN_ALPHA = (2.0 * DEPTH) ** 0.25

ROW_PAD = V7X_SUBLANES
TOKEN_TILE = 512


def _compiler_params(semantics):
    return pltpu.CompilerParams(dimension_semantics=semantics, vmem_limit_bytes=VMEM_LIMIT_BYTES)


def _const_spec(shape):
    nd = len(shape)
    return pl.BlockSpec(shape, lambda *_: (0,) * nd, pipeline_mode=pl.Buffered(1))


def _split_factors(seq_len):
    n2 = 64
    assert seq_len % n2 == 0
    return seq_len // n2, n2


def _dft_tables(n1, n2):
    seq = n1 * n2
    i1 = np.arange(n1)
    th = np.pi * np.outer(i1, i1) / n1
    f1 = np.zeros((2 * n1, n1))
    f1[:n1] = np.cos(th)
    f1[n1] = (-1.0) ** i1
    f1[n1 + 1:] = -np.sin(th[1:])
    g = np.zeros((n1, 2 * n1))
    g[:, :n1] = np.cos(th.T)
    g[:, n1] = (-1.0) ** i1
    g[:, n1 + 1:] = -np.sin(th.T[:, 1:])
    i2 = np.arange(n2)
    base = 2 * np.pi * np.outer(i2, i2) / n2
    m = np.zeros((n1, 2 * n2, 2 * n2))
    mi = np.zeros((n1, 2 * n2, 2 * n2))
    for k in range(1, n1):
        phi = base + 2 * np.pi * k * i2[None, :] / (2 * seq)
        c, s = np.cos(phi), np.sin(phi)
        m[k] = np.block([[c, s], [-s, c]])
        mi[k] = np.block([[c.T, -s.T], [s.T, c.T]]) / seq
    phin = base + np.pi * i2[None, :] / n2
    z = np.zeros((n2, n2))
    ms = np.block([[np.cos(base), z], [-np.sin(base), z], [z, np.cos(phin)], [z, -np.sin(phin)]])
    mis = np.block([[np.cos(base).T, -np.sin(base).T, z, z],
                    [z, z, np.cos(phin).T, -np.sin(phin).T]]) / (2 * seq)
    return dict(f1=f1, g=g, m=m, mi=mi, ms=ms, mis=mis)


def _perm_positions(n1, n2):
    return (np.arange(n2)[:, None] + n2 * np.arange(n1)[None, :]).reshape(-1)


def _positional_features(seq_len, n1, n2):
    pos = _perm_positions(n1, n2).astype(np.float64)
    t = pos / (seq_len - 1)
    w = 2.0 * math.pi * pos / seq_len
    f = np.linspace(1e-4, POS_BANDS - 1, POS_BANDS)
    z = np.concatenate([t[:, None], np.cos(f[None] * w[:, None]), -np.sin(f[None] * w[:, None])], axis=-1)
    zp = np.zeros((seq_len, 40))
    zp[:, :z.shape[1]] = z
    return zp, t[:, None]


def _layer_norm(x, g, b):
    mu = jnp.mean(x, axis=-1, keepdims=True)
    xc = x - mu
    var = jnp.mean(xc * xc, axis=-1, keepdims=True)
    return xc * lax.rsqrt(var + LN_EPS) * g + b


def _dot(a, b):
    return jnp.dot(a, b, preferred_element_type=F32)


def _split_bf16(x):
    hi = x.astype(BF16)
    return hi, (x - hi.astype(F32)).astype(BF16)


def _dot3(a, b):
    a_hi, a_lo = _split_bf16(a)
    b_hi, b_lo = _split_bf16(b)
    return _dot(a_hi, b_hi) + (_dot(a_hi, b_lo) + _dot(a_lo, b_hi))


def _store_slabs(dst_ref, row0, val):
    rows = val.shape[0]
    for j in range(dst_ref.shape[0]):
        dst_ref[j, pl.ds(row0, rows), :] = val[:, j * V7X_LANES:(j + 1) * V7X_LANES]


def _gather_pair(src_ref, start_a, start_b, count, pitch):
    cols = []
    for j in range(src_ref.shape[0]):
        a = src_ref[j, pl.ds(start_a, count, stride=pitch), :]
        b = src_ref[j, pl.ds(start_b, count, stride=pitch), :]
        cols.append(jnp.concatenate([a, b], axis=0))
    return cols[0] if len(cols) == 1 else jnp.concatenate(cols, axis=1)


def _complex_mul(x, k, half):
    xr, xi = x[:half], x[half:]
    kr, ki = k[:half], k[half:]
    return jnp.concatenate([xr * kr - xi * ki, xr * ki + xi * kr], axis=0)


def _row_permutation(n1, j_per):
    i, j = np.meshgrid(np.arange(n1), np.arange(j_per), indexing="ij")
    p = np.zeros((n1 * j_per, n1 * j_per), np.float32)
    p[(j * n1 + i).ravel(), (i * j_per + j).ravel()] = 1.0
    return p


def _inproj_kernel(x_ref, perm_ref, g_ref, b_ref, w_ref, bias_ref, a_ref, u_ref, gate_ref, h_scr, *, chunk):
    n1, j_per, d = x_ref.shape
    for i in range(0, n1, V7X_SUBLANES):
        x = x_ref[i:i + V7X_SUBLANES].reshape(V7X_SUBLANES * j_per, d)
        h_scr[i * j_per:(i + V7X_SUBLANES) * j_per, :] = _layer_norm(x, g_ref[...], b_ref[...]).astype(BF16)
    h = _dot(perm_ref[...], h_scr[...]).astype(BF16)
    wa, wu = a_ref.shape[1], u_ref.shape[1]
    total = w_ref.shape[1]
    for c in range(0, total, chunk):
        y = _dot(h, w_ref[:, c:c + chunk]) + bias_ref[:, c:c + chunk]
        if c < wa:
            a_ref[:, c:c + chunk] = y
        elif c < wa + wu:
            u_ref[:, c - wa:c - wa + chunk] = y
        else:
            gate_ref[:, c - wa - wu:c - wa - wu + chunk] = jax.nn.sigmoid(y)


def _inproj(xv, perm, ln_g, ln_b, w_in, b_in, *, n1, n2, widths):
    bsz = xv.shape[0]
    d = w_in.shape[0]
    seq = n1 * n2
    j_per = TOKEN_TILE // n1
    wa, wu, wg = widths
    chunk = 512
    assert wa % chunk == 0 and wu % chunk == 0 and wg % chunk == 0 and n2 % j_per == 0
    out_shapes = tuple(jax.ShapeDtypeStruct((bsz, seq, w), F32) for w in widths)
    tok = lambda w: pl.BlockSpec((None, TOKEN_TILE, w), lambda b, t: (b, t, 0))
    return pl.pallas_call(
        functools.partial(_inproj_kernel, chunk=chunk),
        grid=(bsz, n2 // j_per),
        in_specs=[
            pl.BlockSpec((None, n1, j_per, d), lambda b, t: (b, 0, t, 0)),
            _const_spec(perm.shape), _const_spec((1, d)), _const_spec((1, d)),
            _const_spec(w_in.shape), _const_spec(b_in.shape),
        ],
        out_specs=[tok(wa), tok(wu), tok(wg)],
        out_shape=out_shapes,
        scratch_shapes=[pltpu.VMEM((TOKEN_TILE, d), BF16)],
        compiler_params=_compiler_params(("parallel", "parallel")),
        name="inproj",
    )(xv, perm, ln_g, ln_b, w_in, b_in)


def _filter_kernel(z_ref, t_ref, w1_ref, b1_ref, q1_ref, w2_ref, b2_ref, q2_ref, wo_ref, dec_ref,
                   f1_ref, m_ref, ms_ref, kf_ref, hid_ref, h_ref, s1_ref, *, n1, n2):
    seq = n1 * n2
    pitch = 2 * n1 + ROW_PAD
    hp = lax.Precision.HIGHEST
    rb = 256

    def mlp(i, c):
        r = pl.multiple_of(i * rb, rb)
        z = z_ref[pl.ds(r, rb), :]
        h = jnp.sin(q1_ref[...] * (jnp.dot(z, w1_ref[...], precision=hp, preferred_element_type=F32) + b1_ref[...]))
        h = jnp.sin(q2_ref[...] * (jnp.dot(h, w2_ref[...], precision=hp, preferred_element_type=F32) + b2_ref[...]))
        hid_ref[pl.ds(r, rb), :] = h
        return c

    @pl.when((pl.program_id(0) == 0) & (pl.program_id(1) == 0))
    def _():
        lax.fori_loop(0, seq // rb, mlp, 0, unroll=4)

    sums = []
    for dr in range(N_DIRS):
        rate = jnp.abs(dec_ref[dr:dr + 1, :])

        def gen(i, acc, dr=dr, rate=rate):
            r = pl.multiple_of(i * rb, rb)
            h = _dot3(hid_ref[pl.ds(r, rb), :], wo_ref[dr])
            h = h * jnp.exp(-t_ref[pl.ds(r, rb), :] * rate)
            if dr == 1:
                row = lax.broadcasted_iota(jnp.int32, h.shape, 0) + r
                h = jnp.where(row == 0, 0.0, h)
            h_ref[dr, pl.ds(r, rb), :] = h
            return acc + jnp.sum(jnp.abs(h), axis=0, keepdims=True)

        sums.append(lax.fori_loop(0, seq // rb, gen, jnp.zeros((1, h_ref.shape[2]), F32), unroll=4))

    denom = sums[0] + sums[1] + L1_EPS
    for dr in range(N_DIRS):
        def stage1(i, c, dr=dr):
            r = pl.multiple_of(i * n1, n1)
            blk = h_ref[dr, pl.ds(r, n1), :] / denom
            _store_slabs(s1_ref, pl.multiple_of(i * pitch, V7X_SUBLANES), _dot(f1_ref[...], blk.astype(BF16)))
            return c

        lax.fori_loop(0, n2, stage1, 0, unroll=8)

        def emit(row0, x, half, dr=dr):
            if dr == 0:
                kf_ref[pl.ds(row0, 2 * half), :] = x
            else:
                sign = jnp.where(lax.broadcasted_iota(jnp.int32, x.shape, 0) < half, 1.0, -1.0)
                kf_ref[pl.ds(row0, 2 * half), :] += sign * x

        b0 = _gather_pair(s1_ref, 0, n1, n2, pitch).astype(BF16)
        x0 = _dot(ms_ref[...], b0)
        emit(0, x0[:2 * n2], n2)
        emit(2 * n2, x0[2 * n2:], n2)

        def stage2(k, c, emit=emit):
            bk = _gather_pair(s1_ref, k, n1 + k, n2, pitch).astype(BF16)
            emit(pl.multiple_of((k + 1) * 2 * n2, 2 * n2), _dot(m_ref[k], bk), n2)
            return c

        lax.fori_loop(1, n1, stage2, 0, unroll=7)


def _filter_spectra(p, tabs, *, n1, n2, width):
    seq = n1 * n2
    cb = 2 * V7X_LANES
    nrow = 2 * n2 * (n1 + 1)
    pitch = 2 * n1 + ROW_PAD
    zfeat, tcol = _positional_features(seq, n1, n2)
    hid = p['w_f2'].shape[0]
    w1 = jnp.zeros((zfeat.shape[1], hid), F32).at[:p['w_f1'].shape[0]].set(p['w_f1'])
    wo = p['w_f_out'].reshape(hid, HYENA_ORDER, N_DIRS, width).transpose(1, 2, 0, 3)
    row = lambda v: v.reshape(1, -1)
    operands = [jnp.asarray(zfeat, F32), jnp.asarray(tcol, F32), w1, row(p['b_f1']), row(p['freq_f1']),
                p['w_f2'], row(p['b_f2']), row(p['freq_f2']), wo,
                p['decay_rate'].reshape(HYENA_ORDER, N_DIRS, width),
                tabs['f1'], tabs['m'], tabs['ms']]
    in_specs = [_const_spec(o.shape) for o in operands]
    in_specs[8] = pl.BlockSpec((None, N_DIRS, hid, cb), lambda c, o: (o, 0, 0, c))
    in_specs[9] = pl.BlockSpec((None, N_DIRS, cb), lambda c, o: (o, 0, c))
    return pl.pallas_call(
        functools.partial(_filter_kernel, n1=n1, n2=n2),
        grid=(width // cb, HYENA_ORDER),
        in_specs=in_specs,
        out_specs=pl.BlockSpec((None, nrow, cb), lambda c, o: (o, 0, c)),
        out_shape=jax.ShapeDtypeStruct((HYENA_ORDER, nrow, width), F32),
        scratch_shapes=[pltpu.VMEM((seq, hid), F32), pltpu.VMEM((N_DIRS, seq, cb), F32),
                        pltpu.VMEM((cb // V7X_LANES, n2 * pitch, V7X_LANES), F32)],
        compiler_params=_compiler_params(("arbitrary", "arbitrary")),
        name="filter_spectra",
    )(*operands)


def _hyena_kernel(x1_ref, x2_ref, v_ref, cw1_ref, cw2_ref, cwv_ref, cb1_ref, cb2_ref, cbv_ref,
                  kf_ref, hb_ref, f1_ref, g_ref, m_ref, mi_ref, ms_ref, mis_ref,
                  z_ref, u_scr, w_scr, s1_ref, s2_ref, *, n1, n2):
    pitch1 = 2 * n1 + ROW_PAD
    pitch2 = 2 * n2 + ROW_PAD
    cbw = z_ref.shape[1]
    rows = lax.broadcasted_iota(jnp.int32, (n1, cbw), 0)

    def short_conv(src, w_ref, b_ref, i):
        r = i * n1
        if i == 0:
            prev = jnp.where(rows == 0, 0.0, src[pl.ds((n2 - 1) * n1 - 1, n1), :])
        else:
            prev = src[pl.ds(r - n1, n1), :]
        if i == n2 - 1:
            nxt = jnp.where(rows == n1 - 1, 0.0, src[pl.ds(1, n1), :])
        else:
            nxt = src[pl.ds(r + n1, n1), :]
        return prev * w_ref[0:1, :] + src[pl.ds(r, n1), :] * w_ref[1:2, :] + nxt * w_ref[2:3, :] + b_ref[...]

    def stage1(i, u):
        _store_slabs(s1_ref, i * pitch1, _dot(f1_ref[...], u.astype(BF16)))

    def middle(o):
        b0 = _gather_pair(s1_ref, 0, n1, n2, pitch1).astype(BF16)
        x0 = _dot(ms_ref[...], b0)
        y0 = jnp.concatenate([_complex_mul(x0[:2 * n2], kf_ref[o, 0:2 * n2, :], n2),
                              _complex_mul(x0[2 * n2:], kf_ref[o, 2 * n2:4 * n2, :], n2)], axis=0)
        _store_slabs(s2_ref, 0, _dot(mis_ref[...], y0.astype(BF16)))
        for k in range(1, n1):
            bk = _gather_pair(s1_ref, k, n1 + k, n2, pitch1).astype(BF16)
            xk = _dot(m_ref[k], bk)
            yk = _complex_mul(xk, kf_ref[o, (k + 1) * 2 * n2:(k + 2) * 2 * n2, :], n2).astype(BF16)
            _store_slabs(s2_ref, k * pitch2, _dot(mi_ref[k], yk))

    def finish(o, gate_ref, gw_ref, gb_ref, u_in, z_out):
        for i in range(n2):
            dk = _gather_pair(s2_ref, i, n2 + i, n1, pitch2).astype(BF16)
            conv = _dot(g_ref[...], dk) + u_in[pl.ds(i * n1, n1), :] * hb_ref[o:o + 1, :]
            z_out[pl.ds(i * n1, n1), :] = short_conv(gate_ref, gw_ref, gb_ref, i) * conv

    for i in range(n2):
        u = short_conv(v_ref, cwv_ref, cbv_ref, i)
        u_scr[pl.ds(i * n1, n1), :] = u
        stage1(i, u)
    middle(0)
    finish(0, x1_ref, cw1_ref, cb1_ref, u_scr, w_scr)
    for i in range(n2):
        stage1(i, w_scr[pl.ds(i * n1, n1), :])
    middle(1)
    finish(1, x2_ref, cw2_ref, cb2_ref, w_scr, z_ref)


def _hyena_block_width(n1, n2):
    seq = n1 * n2
    nrow = 2 * n2 * (n1 + 1)

    def vmem_bytes(cb):
        windows = 2 * 4 * seq * cb
        scratch = 2 * seq * cb + n2 * (2 * n1 + ROW_PAD) * cb + n1 * (2 * n2 + ROW_PAD) * cb
        tables = 2 * (2 * n1 * (2 * n2) ** 2 + 2 * 4 * n2 * 2 * n2 + 2 * 2 * n1 * n1) // 4
        return 4 * (windows + scratch + HYENA_ORDER * nrow * cb + tables)

    return next(c for c in (2 * V7X_LANES, V7X_LANES) if vmem_bytes(c) <= VMEM_LIMIT_BYTES)


def _hyena(u3, conv_w, conv_b, kf, hyena_bias, tabs, *, n1, n2, width):
    cb = _hyena_block_width(n1, n2)
    bsz, seq, _ = u3.shape
    ncb = width // cb
    nrow = kf.shape[1]
    pitch1 = 2 * n1 + ROW_PAD
    pitch2 = 2 * n2 + ROW_PAD
    nslab = cb // V7X_LANES
    seq_spec = lambda part: pl.BlockSpec((None, seq, cb), lambda c, b: (b, 0, part * ncb + c))
    tap_spec = lambda part: pl.BlockSpec((conv_w.shape[0], cb), lambda c, b: (0, part * ncb + c))
    bias_spec = lambda part: pl.BlockSpec((1, cb), lambda c, b: (0, part * ncb + c))
    consts = [tabs[k] for k in ('f1', 'g', 'm', 'mi', 'ms', 'mis')]
    return pl.pallas_call(
        functools.partial(_hyena_kernel, n1=n1, n2=n2),
        grid=(ncb, bsz),
        in_specs=[seq_spec(0), seq_spec(1), seq_spec(2),
                  tap_spec(0), tap_spec(1), tap_spec(2), bias_spec(0), bias_spec(1), bias_spec(2),
                  pl.BlockSpec((HYENA_ORDER, nrow, cb), lambda c, b: (0, 0, c), pipeline_mode=pl.Buffered(1)),
                  pl.BlockSpec((HYENA_ORDER, cb), lambda c, b: (0, c)),
                  *[_const_spec(t.shape) for t in consts]],
        out_specs=pl.BlockSpec((None, seq, cb), lambda c, b: (b, 0, c)),
        out_shape=jax.ShapeDtypeStruct((bsz, seq, width), F32),
        scratch_shapes=[pltpu.VMEM((seq, cb), F32), pltpu.VMEM((seq, cb), F32),
                        pltpu.VMEM((nslab, n2 * pitch1, V7X_LANES), F32),
                        pltpu.VMEM((nslab, n1 * pitch2, V7X_LANES), F32)],
        compiler_params=_compiler_params(("parallel", "parallel")),
        name="hyena",
    )(u3, u3, u3, conv_w, conv_w, conv_w, conv_b, conv_b, conv_b, kf, hyena_bias, *consts)


def _pool_tile(prev_ref, cur_ref, next_ref, *, n1, n2):
    j_per = cur_ref.shape[0] // n1
    assert POOL_HALO <= j_per
    seq = n1 * n2
    t = pl.program_id(1)
    first, last = t == 0, t == pl.num_programs(1) - 1
    rows = lax.broadcasted_iota(jnp.int32, (n1, V7X_LANES), 0)
    cols = []
    for g, win in enumerate(POOL_WINDOWS):
        lo = win // 2
        hi = win - 1 - lo
        lanes = slice(g * V7X_LANES, (g + 1) * V7X_LANES)
        memo = {}

        def blk(m):
            if m not in memo:
                if m < 0:
                    b = prev_ref[(j_per + m) * n1:(j_per + m + 1) * n1, lanes]
                    memo[m] = jnp.where(first, jnp.where(rows == 0, 0.0, pltpu.roll(b, 1, 0)), b)
                elif m >= j_per:
                    b = next_ref[(m - j_per) * n1:(m - j_per + 1) * n1, lanes]
                    memo[m] = jnp.where(last, jnp.where(rows == n1 - 1, 0.0, pltpu.roll(b, n1 - 1, 0)), b)
                else:
                    memo[m] = cur_ref[m * n1:(m + 1) * n1, lanes]
            return memo[m]

        outs = []
        for j in range(j_per):
            acc = blk(j - lo)
            for dlt in range(-lo + 1, hi + 1):
                acc = acc + blk(j + dlt)
            pos = rows * n2 + (t * j_per + j)
            cnt = jnp.minimum(pos + (win - lo), seq) - jnp.maximum(pos - lo, 0)
            outs.append(acc / cnt.astype(F32) - blk(j))
        cols.append(jnp.concatenate(outs, axis=0))
    return jnp.concatenate(cols, axis=1)


def _merge_ffn_kernel(x_ref, permt_ref, ap_ref, ac_ref, an_ref, z_ref, gate_ref, lng_ref, lnb_ref, wp_ref, bp_ref,
                      ps_ref, wpp_ref, whp_ref, wo_ref, bo_ref, g1_ref, b1_ref, w1_ref, bf1_ref, w2_ref, bf2_ref,
                      g2_ref, b2_ref, y_ref, *, n1, n2, ffn_chunk):
    tm, d = z_ref.shape
    ngrp = wp_ref.shape[0]
    gd = wp_ref.shape[1]
    pm = _pool_tile(ap_ref, ac_ref, an_ref, n1=n1, n2=n2).astype(BF16)
    p = jnp.concatenate([_dot(pm[:, g * gd:(g + 1) * gd], wp_ref[g]) for g in range(ngrp)], axis=1)
    p = (p + bp_ref[...]) * ps_ref[...]
    ya = _dot(p.astype(BF16), wpp_ref[...])
    yb = _dot(z_ref[...].astype(BF16), whp_ref[...])
    m = gate_ref[:, :d] * ya + gate_ref[:, d:] * yb
    m = _dot(permt_ref[...], m.astype(BF16)).astype(BF16)
    mo = _dot(m, wo_ref[...]) + bo_ref[...]
    h0 = _layer_norm(x_ref[...].reshape(tm, d), lng_ref[...], lnb_ref[...])
    h = _layer_norm(DN_ALPHA * h0 + mo, g1_ref[...], b1_ref[...])
    hb = h.astype(BF16)
    acc = jnp.zeros(h.shape, F32)
    for c in range(0, w1_ref.shape[1], ffn_chunk):
        u = jnp.maximum(_dot(hb, w1_ref[:, c:c + ffn_chunk]) + bf1_ref[:, c:c + ffn_chunk], 0.0)
        acc = acc + _dot((u * u).astype(BF16), w2_ref[c:c + ffn_chunk, :])
    y = _layer_norm(DN_ALPHA * h + acc + bf2_ref[...], g2_ref[...], b2_ref[...])
    y_ref[...] = y.reshape(y_ref.shape)


def _merge_ffn(xv, permt, a, z, gate, consts, *, n1, n2):
    bsz, seq, d = z.shape
    j_per = TOKEN_TILE // n1
    ntile = n2 // j_per
    tok = lambda w: pl.BlockSpec((None, TOKEN_TILE, w), lambda b, t: (b, t, 0))
    nat = pl.BlockSpec((None, n1, j_per, d), lambda b, t: (b, 0, t, 0))
    a_prev = pl.BlockSpec((None, TOKEN_TILE, a.shape[2]), lambda b, t: (b, (t + ntile - 1) % ntile, 0))
    a_next = pl.BlockSpec((None, TOKEN_TILE, a.shape[2]), lambda b, t: (b, (t + 1) % ntile, 0))
    return pl.pallas_call(
        functools.partial(_merge_ffn_kernel, n1=n1, n2=n2, ffn_chunk=1024),
        grid=(bsz, ntile),
        in_specs=[nat, _const_spec(permt.shape), a_prev, tok(a.shape[2]), a_next, tok(d), tok(gate.shape[2]),
                  *[_const_spec(c.shape) for c in consts]],
        out_specs=nat,
        out_shape=jax.ShapeDtypeStruct(xv.shape, F32),
        compiler_params=_compiler_params(("parallel", "parallel")),
        name="merge_ffn",
    )(xv, permt, a, a, a, z, gate, *consts)


def _encoder(x, p):
    bsz, seq, d = x.shape
    n1, n2 = _split_factors(seq)
    pool_w = p['w_pool_proj'].shape[0]
    hy_w = p['w_hyena_proj'].shape[0]
    row = lambda v: v.reshape(1, -1).astype(F32)
    tabs = {k: jnp.asarray(v, BF16) for k, v in _dft_tables(n1, n2).items()}

    xv = x.reshape(bsz, n1, n2, d)
    perm = _row_permutation(n1, TOKEN_TILE // n1)
    a, u3, gate = _inproj(xv, jnp.asarray(perm, BF16), row(p['ln_in_g']), row(p['ln_in_b']),
                          p['w_in'].astype(BF16), row(p['b_in']),
                          n1=n1, n2=n2, widths=(pool_w, HYENA_ORDER * hy_w + hy_w, 2 * d))
    kf = _filter_spectra(p, tabs, n1=n1, n2=n2, width=hy_w)
    z = _hyena(u3, p['conv_w'], row(p['conv_b']), kf, p['hyena_bias'], tabs, n1=n1, n2=n2, width=hy_w)
    merge_consts = [row(p['ln_in_g']), row(p['ln_in_b']), p['w_pool'].astype(BF16), row(p['b_pool']),
                    row(p['pool_scale']), p['w_pool_proj'].astype(BF16), p['w_hyena_proj'].astype(BF16),
                    p['w_o'].astype(BF16), row(p['b_o']), row(p['ln1_g']), row(p['ln1_b']),
                    p['w_ff1'].astype(BF16), row(p['b_ff1']), p['w_ff2'].astype(BF16), row(p['b_ff2']),
                    row(p['ln2_g']), row(p['ln2_b'])]
    y = _merge_ffn(xv, jnp.asarray(perm.T, BF16), a, z, gate, merge_consts, n1=n1, n2=n2)
    return y.reshape(bsz, seq, d)


def kernel(x_prompt, x_sample, ln_in_g, ln_in_b, w_in, b_in, w_pool, b_pool, pool_scale, w_pool_proj, conv_w, conv_b, w_f1, b_f1, freq_f1, w_f2, b_f2, freq_f2, w_f_out, decay_rate, hyena_bias, w_hyena_proj, w_o, b_o, ln1_g, ln1_b, w_ff1, b_ff1, w_ff2, b_ff2, ln2_g, ln2_b):
    layer = dict(w_in=w_in, b_in=b_in, w_pool=w_pool, b_pool=b_pool, pool_scale=pool_scale,
                 w_pool_proj=w_pool_proj, conv_w=conv_w, conv_b=conv_b, w_f1=w_f1, b_f1=b_f1, freq_f1=freq_f1,
                 w_f2=w_f2, b_f2=b_f2, freq_f2=freq_f2, w_f_out=w_f_out, decay_rate=decay_rate,
                 hyena_bias=hyena_bias, w_hyena_proj=w_hyena_proj, w_o=w_o, b_o=b_o, ln1_g=ln1_g, ln1_b=ln1_b,
                 w_ff1=w_ff1, b_ff1=b_ff1, w_ff2=w_ff2, b_ff2=b_ff2, ln2_g=ln2_g, ln2_b=ln2_b)
    assert all(v.shape[0] == DEPTH for v in layer.values())
    p = {k: v[0] for k, v in layer.items()}
    p.update(ln_in_g=ln_in_g, ln_in_b=ln_in_b)
    y_prompt = _encoder(x_prompt, p)
    y_sample = _encoder(x_sample, p)
    return (y_prompt, y_sample)
```

```python
import functools
import math

import numpy as np
import jax
import jax.numpy as jnp
from jax import lax
from jax.experimental import pallas as pl
from jax.experimental.pallas import tpu as pltpu

F32 = jnp.float32
BF16 = jnp.bfloat16

V7X_LANES = 128
V7X_SUBLANES = 8
V7X_VMEM_BYTES = 64 * 1024 * 1024
VMEM_LIMIT_BYTES = V7X_VMEM_BYTES - 8 * 1024 * 1024

POOL_WINDOWS = (2, 4, 8, 16)
POOL_HALO = max(max(w // 2, w - 1 - w // 2) for w in POOL_WINDOWS)
N_DIRS = 2
HYENA_ORDER = 2
POS_BANDS = 16
LN_EPS = 1e-5
L1_EPS = 1e-6
DEPTH = 1
DN_ALPHA = (2.0 * DEPTH) ** 0.25

ROW_PAD = V7X_SUBLANES
TOKEN_TILE = 512


def _compiler_params(semantics):
    return pltpu.CompilerParams(dimension_semantics=semantics, vmem_limit_bytes=VMEM_LIMIT_BYTES)


def _const_spec(shape):
    nd = len(shape)
    return pl.BlockSpec(shape, lambda *_: (0,) * nd, pipeline_mode=pl.Buffered(1))


def _split_factors(seq_len):
    n2 = 64
    assert seq_len % n2 == 0
    return seq_len // n2, n2


def _dft_tables(n1, n2):
    seq = n1 * n2
    i1 = np.arange(n1)
    th = np.pi * np.outer(i1, i1) / n1
    f1 = np.zeros((2 * n1, n1))
    f1[:n1] = np.cos(th)
    f1[n1] = (-1.0) ** i1
    f1[n1 + 1:] = -np.sin(th[1:])
    g = np.zeros((n1, 2 * n1))
    g[:, :n1] = np.cos(th.T)
    g[:, n1] = (-1.0) ** i1
    g[:, n1 + 1:] = -np.sin(th.T[:, 1:])
    i2 = np.arange(n2)
    base = 2 * np.pi * np.outer(i2, i2) / n2
    m = np.zeros((n1, 2 * n2, 2 * n2))
    mi = np.zeros((n1, 2 * n2, 2 * n2))
    for k in range(1, n1):
        phi = base + 2 * np.pi * k * i2[None, :] / (2 * seq)
        c, s = np.cos(phi), np.sin(phi)
        m[k] = np.block([[c, s], [-s, c]])
        mi[k] = np.block([[c.T, -s.T], [s.T, c.T]]) / seq
    phin = base + np.pi * i2[None, :] / n2
    z = np.zeros((n2, n2))
    ms = np.block([[np.cos(base), z], [-np.sin(base), z], [z, np.cos(phin)], [z, -np.sin(phin)]])
    mis = np.block([[np.cos(base).T, -np.sin(base).T, z, z],
                    [z, z, np.cos(phin).T, -np.sin(phin).T]]) / (2 * seq)
    return dict(f1=f1, g=g, m=m, mi=mi, ms=ms, mis=mis)


def _perm_positions(n1, n2):
    return (np.arange(n2)[:, None] + n2 * np.arange(n1)[None, :]).reshape(-1)


def _positional_features(seq_len, n1, n2):
    pos = _perm_positions(n1, n2).astype(np.float64)
    t = pos / (seq_len - 1)
    w = 2.0 * math.pi * pos / seq_len
    f = np.linspace(1e-4, POS_BANDS - 1, POS_BANDS)
    z = np.concatenate([t[:, None], np.cos(f[None] * w[:, None]), -np.sin(f[None] * w[:, None])], axis=-1)
    zp = np.zeros((seq_len, 40))
    zp[:, :z.shape[1]] = z
    return zp, t[:, None]


def _layer_norm(x, g, b):
    mu = jnp.mean(x, axis=-1, keepdims=True)
    xc = x - mu
    var = jnp.mean(xc * xc, axis=-1, keepdims=True)
    return xc * lax.rsqrt(var + LN_EPS) * g + b


def _dot(a, b):
    return jnp.dot(a, b, preferred_element_type=F32)


def _split_bf16(x):
    hi = x.astype(BF16)
    return hi, (x - hi.astype(F32)).astype(BF16)


def _dot3(a, b):
    a_hi, a_lo = _split_bf16(a)
    b_hi, b_lo = _split_bf16(b)
    return _dot(a_hi, b_hi) + (_dot(a_hi, b_lo) + _dot(a_lo, b_hi))


def _store_slabs(dst_ref, row0, val):
    rows = val.shape[0]
    for j in range(dst_ref.shape[0]):
        dst_ref[j, pl.ds(row0, rows), :] = val[:, j * V7X_LANES:(j + 1) * V7X_LANES]


def _gather_pair(src_ref, start_a, start_b, count, pitch):
    cols = []
    for j in range(src_ref.shape[0]):
        a = src_ref[j, pl.ds(start_a, count, stride=pitch), :]
        b = src_ref[j, pl.ds(start_b, count, stride=pitch), :]
        cols.append(jnp.concatenate([a, b], axis=0))
    return cols[0] if len(cols) == 1 else jnp.concatenate(cols, axis=1)


def _complex_mul(x, k, half):
    xr, xi = x[:half], x[half:]
    kr, ki = k[:half], k[half:]
    return jnp.concatenate([xr * kr - xi * ki, xr * ki + xi * kr], axis=0)


def _row_permutation(n1, j_per):
    i, j = np.meshgrid(np.arange(n1), np.arange(j_per), indexing="ij")
    p = np.zeros((n1 * j_per, n1 * j_per), np.float32)
    p[(j * n1 + i).ravel(), (i * j_per + j).ravel()] = 1.0
    return p


def _inproj_kernel(x_ref, perm_ref, g_ref, b_ref, w_ref, bias_ref, a_ref, u_ref, gate_ref, h_scr, *, chunk):
    n1, j_per, d = x_ref.shape
    for i in range(0, n1, V7X_SUBLANES):
        x = x_ref[i:i + V7X_SUBLANES].reshape(V7X_SUBLANES * j_per, d)
        h_scr[i * j_per:(i + V7X_SUBLANES) * j_per, :] = _layer_norm(x, g_ref[...], b_ref[...]).astype(BF16)
    h = _dot(perm_ref[...], h_scr[...]).astype(BF16)
    wa, wu = a_ref.shape[1], u_ref.shape[1]
    total = w_ref.shape[1]
    for c in range(0, total, chunk):
        y = _dot(h, w_ref[:, c:c + chunk]) + bias_ref[:, c:c + chunk]
        if c < wa:
            a_ref[:, c:c + chunk] = y
        elif c < wa + wu:
            u_ref[:, c - wa:c - wa + chunk] = y
        else:
            gate_ref[:, c - wa - wu:c - wa - wu + chunk] = jax.nn.sigmoid(y)


def _inproj(xv, perm, ln_g, ln_b, w_in, b_in, *, n1, n2, widths):
    bsz = xv.shape[0]
    d = w_in.shape[0]
    seq = n1 * n2
    j_per = TOKEN_TILE // n1
    wa, wu, wg = widths
    chunk = 512
    assert wa % chunk == 0 and wu % chunk == 0 and wg % chunk == 0 and n2 % j_per == 0
    out_shapes = tuple(jax.ShapeDtypeStruct((bsz, seq, w), F32) for w in widths)
    tok = lambda w: pl.BlockSpec((None, TOKEN_TILE, w), lambda b, t: (b, t, 0))
    return pl.pallas_call(
        functools.partial(_inproj_kernel, chunk=chunk),
        grid=(bsz, n2 // j_per),
        in_specs=[
            pl.BlockSpec((None, n1, j_per, d), lambda b, t: (b, 0, t, 0)),
            _const_spec(perm.shape), _const_spec((1, d)), _const_spec((1, d)),
            _const_spec(w_in.shape), _const_spec(b_in.shape),
        ],
        out_specs=[tok(wa), tok(wu), tok(wg)],
        out_shape=out_shapes,
        scratch_shapes=[pltpu.VMEM((TOKEN_TILE, d), BF16)],
        compiler_params=_compiler_params(("parallel", "parallel")),
        name="inproj",
    )(xv, perm, ln_g, ln_b, w_in, b_in)


def _filter_kernel(z_ref, t_ref, w1_ref, b1_ref, q1_ref, w2_ref, b2_ref, q2_ref, wo_ref, dec_ref,
                   f1_ref, m_ref, ms_ref, kf_ref, hid_ref, h_ref, s1_ref, *, n1, n2):
    seq = n1 * n2
    pitch = 2 * n1 + ROW_PAD
    hp = lax.Precision.HIGHEST
    rb = 256

    def mlp(i, c):
        r = pl.multiple_of(i * rb, rb)
        z = z_ref[pl.ds(r, rb), :]
        h = jnp.sin(q1_ref[...] * (jnp.dot(z, w1_ref[...], precision=hp, preferred_element_type=F32) + b1_ref[...]))
        h = jnp.sin(q2_ref[...] * (jnp.dot(h, w2_ref[...], precision=hp, preferred_element_type=F32) + b2_ref[...]))
        hid_ref[pl.ds(r, rb), :] = h
        return c

    @pl.when((pl.program_id(0) == 0) & (pl.program_id(1) == 0))
    def _():
        lax.fori_loop(0, seq // rb, mlp, 0, unroll=4)

    sums = []
    for dr in range(N_DIRS):
        rate = jnp.abs(dec_ref[dr:dr + 1, :])

        def gen(i, acc, dr=dr, rate=rate):
            r = pl.multiple_of(i * rb, rb)
            h = _dot3(hid_ref[pl.ds(r, rb), :], wo_ref[dr])
            h = h * jnp.exp(-t_ref[pl.ds(r, rb), :] * rate)
            if dr == 1:
                row = lax.broadcasted_iota(jnp.int32, h.shape, 0) + r
                h = jnp.where(row == 0, 0.0, h)
            h_ref[dr, pl.ds(r, rb), :] = h
            return acc + jnp.sum(jnp.abs(h), axis=0, keepdims=True)

        sums.append(lax.fori_loop(0, seq // rb, gen, jnp.zeros((1, h_ref.shape[2]), F32), unroll=8))

    denom = sums[0] + sums[1] + L1_EPS
    for dr in range(N_DIRS):
        def stage1(i, c, dr=dr):
            r = pl.multiple_of(i * n1, n1)
            blk = h_ref[dr, pl.ds(r, n1), :] / denom
            _store_slabs(s1_ref, pl.multiple_of(i * pitch, V7X_SUBLANES), _dot(f1_ref[...], blk.astype(BF16)))
            return c

        lax.fori_loop(0, n2, stage1, 0, unroll=16)

        def emit(row0, x, half, dr=dr):
            if dr == 0:
                kf_ref[pl.ds(row0, 2 * half), :] = x
            else:
                sign = jnp.where(lax.broadcasted_iota(jnp.int32, x.shape, 0) < half, 1.0, -1.0)
                kf_ref[pl.ds(row0, 2 * half), :] += sign * x

        b0 = _gather_pair(s1_ref, 0, n1, n2, pitch).astype(BF16)
        x0 = _dot(ms_ref[...], b0)
        emit(0, x0[:2 * n2], n2)
        emit(2 * n2, x0[2 * n2:], n2)

        def stage2(k, c, emit=emit):
            bk = _gather_pair(s1_ref, k, n1 + k, n2, pitch).astype(BF16)
            emit(pl.multiple_of((k + 1) * 2 * n2, 2 * n2), _dot(m_ref[k], bk), n2)
            return c

        lax.fori_loop(1, n1, stage2, 0, unroll=21)


def _filter_spectra(p, tabs, *, n1, n2, width):
    seq = n1 * n2
    cb = 2 * V7X_LANES
    nrow = 2 * n2 * (n1 + 1)
    pitch = 2 * n1 + ROW_PAD
    zfeat, tcol = _positional_features(seq, n1, n2)
    hid = p['w_f2'].shape[0]
    w1 = jnp.zeros((zfeat.shape[1], hid), F32).at[:p['w_f1'].shape[0]].set(p['w_f1'])
    wo = p['w_f_out'].reshape(hid, HYENA_ORDER, N_DIRS, width).transpose(1, 2, 0, 3)
    row = lambda v: v.reshape(1, -1)
    operands = [jnp.asarray(zfeat, F32), jnp.asarray(tcol, F32), w1, row(p['b_f1']), row(p['freq_f1']),
                p['w_f2'], row(p['b_f2']), row(p['freq_f2']), wo,
                p['decay_rate'].reshape(HYENA_ORDER, N_DIRS, width),
                tabs['f1'], tabs['m'], tabs['ms']]
    in_specs = [_const_spec(o.shape) for o in operands]
    in_specs[8] = pl.BlockSpec((None, N_DIRS, hid, cb), lambda c, o: (o, 0, 0, c))
    in_specs[9] = pl.BlockSpec((None, N_DIRS, cb), lambda c, o: (o, 0, c))
    return pl.pallas_call(
        functools.partial(_filter_kernel, n1=n1, n2=n2),
        grid=(width // cb, HYENA_ORDER),
        in_specs=in_specs,
        out_specs=pl.BlockSpec((None, nrow, cb), lambda c, o: (o, 0, c)),
        out_shape=jax.ShapeDtypeStruct((HYENA_ORDER, nrow, width), F32),
        scratch_shapes=[pltpu.VMEM((seq, hid), F32), pltpu.VMEM((N_DIRS, seq, cb), F32),
                        pltpu.VMEM((cb // V7X_LANES, n2 * pitch, V7X_LANES), F32)],
        compiler_params=_compiler_params(("arbitrary", "arbitrary")),
        name="filter_spectra",
    )(*operands)


def _hyena_kernel(x1_ref, x2_ref, v_ref, cw1_ref, cw2_ref, cwv_ref, cb1_ref, cb2_ref, cbv_ref,
                  kf_ref, hb_ref, f1_ref, g_ref, m_ref, mi_ref, ms_ref, mis_ref,
                  z_ref, u_scr, w_scr, s1_ref, s2_ref, *, n1, n2):
    pitch1 = 2 * n1 + ROW_PAD
    pitch2 = 2 * n2 + ROW_PAD
    cbw = z_ref.shape[1]
    rows = lax.broadcasted_iota(jnp.int32, (n1, cbw), 0)

    def short_conv(src, w_ref, b_ref, i):
        r = i * n1
        if i == 0:
            prev = jnp.where(rows == 0, 0.0, src[pl.ds((n2 - 1) * n1 - 1, n1), :])
        else:
            prev = src[pl.ds(r - n1, n1), :]
        if i == n2 - 1:
            nxt = jnp.where(rows == n1 - 1, 0.0, src[pl.ds(1, n1), :])
        else:
            nxt = src[pl.ds(r + n1, n1), :]
        return prev * w_ref[0:1, :] + src[pl.ds(r, n1), :] * w_ref[1:2, :] + nxt * w_ref[2:3, :] + b_ref[...]

    def stage1(i, u):
        _store_slabs(s1_ref, i * pitch1, _dot(f1_ref[...], u.astype(BF16)))

    def middle(o):
        b0 = _gather_pair(s1_ref, 0, n1, n2, pitch1).astype(BF16)
        x0 = _dot(ms_ref[...], b0)
        y0 = jnp.concatenate([_complex_mul(x0[:2 * n2], kf_ref[o, 0:2 * n2, :], n2),
                              _complex_mul(x0[2 * n2:], kf_ref[o, 2 * n2:4 * n2, :], n2)], axis=0)
        _store_slabs(s2_ref, 0, _dot(mis_ref[...], y0.astype(BF16)))
        for k in range(1, n1):
            bk = _gather_pair(s1_ref, k, n1 + k, n2, pitch1).astype(BF16)
            xk = _dot(m_ref[k], bk)
            yk = _complex_mul(xk, kf_ref[o, (k + 1) * 2 * n2:(k + 2) * 2 * n2, :], n2).astype(BF16)
            _store_slabs(s2_ref, k * pitch2, _dot(mi_ref[k], yk))

    def finish(o, gate_ref, gw_ref, gb_ref, u_in, z_out):
        for i in range(n2):
            dk = _gather_pair(s2_ref, i, n2 + i, n1, pitch2).astype(BF16)
            conv = _dot(g_ref[...], dk) + u_in[pl.ds(i * n1, n1), :] * hb_ref[o:o + 1, :]
            z_out[pl.ds(i * n1, n1), :] = short_conv(gate_ref, gw_ref, gb_ref, i) * conv

    for i in range(n2):
        u = short_conv(v_ref, cwv_ref, cbv_ref, i)
        u_scr[pl.ds(i * n1, n1), :] = u
        stage1(i, u)
    middle(0)
    finish(0, x1_ref, cw1_ref, cb1_ref, u_scr, w_scr)
    for i in range(n2):
        stage1(i, w_scr[pl.ds(i * n1, n1), :])
    middle(1)
    finish(1, x2_ref, cw2_ref, cb2_ref, w_scr, z_ref)


def _hyena_block_width(n1, n2):
    seq = n1 * n2
    nrow = 2 * n2 * (n1 + 1)

    def vmem_bytes(cb):
        windows = 2 * 4 * seq * cb
        scratch = 2 * seq * cb + n2 * (2 * n1 + ROW_PAD) * cb + n1 * (2 * n2 + ROW_PAD) * cb
        tables = 2 * (2 * n1 * (2 * n2) ** 2 + 2 * 4 * n2 * 2 * n2 + 2 * 2 * n1 * n1) // 4
        return 4 * (windows + scratch + HYENA_ORDER * nrow * cb + tables)

    return next(c for c in (2 * V7X_LANES, V7X_LANES) if vmem_bytes(c) <= VMEM_LIMIT_BYTES)


def _hyena(u3, conv_w, conv_b, kf, hyena_bias, tabs, *, n1, n2, width):
    cb = _hyena_block_width(n1, n2)
    bsz, seq, _ = u3.shape
    ncb = width // cb
    nrow = kf.shape[1]
    pitch1 = 2 * n1 + ROW_PAD
    pitch2 = 2 * n2 + ROW_PAD
    nslab = cb // V7X_LANES
    seq_spec = lambda part: pl.BlockSpec((None, seq, cb), lambda c, b: (b, 0, part * ncb + c))
    tap_spec = lambda part: pl.BlockSpec((conv_w.shape[0], cb), lambda c, b: (0, part * ncb + c))
    bias_spec = lambda part: pl.BlockSpec((1, cb), lambda c, b: (0, part * ncb + c))
    consts = [tabs[k] for k in ('f1', 'g', 'm', 'mi', 'ms', 'mis')]
    return pl.pallas_call(
        functools.partial(_hyena_kernel, n1=n1, n2=n2),
        grid=(ncb, bsz),
        in_specs=[seq_spec(0), seq_spec(1), seq_spec(2),
                  tap_spec(0), tap_spec(1), tap_spec(2), bias_spec(0), bias_spec(1), bias_spec(2),
                  pl.BlockSpec((HYENA_ORDER, nrow, cb), lambda c, b: (0, 0, c), pipeline_mode=pl.Buffered(1)),
                  pl.BlockSpec((HYENA_ORDER, cb), lambda c, b: (0, c)),
                  *[_const_spec(t.shape) for t in consts]],
        out_specs=pl.BlockSpec((None, seq, cb), lambda c, b: (b, 0, c)),
        out_shape=jax.ShapeDtypeStruct((bsz, seq, width), F32),
        scratch_shapes=[pltpu.VMEM((seq, cb), F32), pltpu.VMEM((seq, cb), F32),
                        pltpu.VMEM((nslab, n2 * pitch1, V7X_LANES), F32),
                        pltpu.VMEM((nslab, n1 * pitch2, V7X_LANES), F32)],
        compiler_params=_compiler_params(("parallel", "parallel")),
        name="hyena",
    )(u3, u3, u3, conv_w, conv_w, conv_w, conv_b, conv_b, conv_b, kf, hyena_bias, *consts)


def _pool_tile(prev_ref, cur_ref, next_ref, *, n1, n2):
    j_per = cur_ref.shape[0] // n1
    assert POOL_HALO <= j_per
    seq = n1 * n2
    t = pl.program_id(1)
    first, last = t == 0, t == pl.num_programs(1) - 1
    rows = lax.broadcasted_iota(jnp.int32, (n1, V7X_LANES), 0)
    cols = []
    for g, win in enumerate(POOL_WINDOWS):
        lo = win // 2
        hi = win - 1 - lo
        lanes = slice(g * V7X_LANES, (g + 1) * V7X_LANES)
        memo = {}

        def blk(m):
            if m not in memo:
                if m < 0:
                    b = prev_ref[(j_per + m) * n1:(j_per + m + 1) * n1, lanes]
                    memo[m] = jnp.where(first, jnp.where(rows == 0, 0.0, pltpu.roll(b, 1, 0)), b)
                elif m >= j_per:
                    b = next_ref[(m - j_per) * n1:(m - j_per + 1) * n1, lanes]
                    memo[m] = jnp.where(last, jnp.where(rows == n1 - 1, 0.0, pltpu.roll(b, n1 - 1, 0)), b)
                else:
                    memo[m] = cur_ref[m * n1:(m + 1) * n1, lanes]
            return memo[m]

        outs = []
        for j in range(j_per):
            acc = blk(j - lo)
            for dlt in range(-lo + 1, hi + 1):
                acc = acc + blk(j + dlt)
            pos = rows * n2 + (t * j_per + j)
            cnt = jnp.minimum(pos + (win - lo), seq) - jnp.maximum(pos - lo, 0)
            outs.append(acc / cnt.astype(F32) - blk(j))
        cols.append(jnp.concatenate(outs, axis=0))
    return jnp.concatenate(cols, axis=1)


def _merge_ffn_kernel(x_ref, permt_ref, ap_ref, ac_ref, an_ref, z_ref, gate_ref, lng_ref, lnb_ref, wp_ref, bp_ref,
                      ps_ref, wpp_ref, whp_ref, wo_ref, bo_ref, g1_ref, b1_ref, w1_ref, bf1_ref, w2_ref, bf2_ref,
                      g2_ref, b2_ref, y_ref, *, n1, n2, ffn_chunk):
    tm, d = z_ref.shape
    ngrp = wp_ref.shape[0]
    gd = wp_ref.shape[1]
    pm = _pool_tile(ap_ref, ac_ref, an_ref, n1=n1, n2=n2).astype(BF16)
    p = jnp.concatenate([_dot(pm[:, g * gd:(g + 1) * gd], wp_ref[g]) for g in range(ngrp)], axis=1)
    p = (p + bp_ref[...]) * ps_ref[...]
    ya = _dot(p.astype(BF16), wpp_ref[...])
    yb = _dot(z_ref[...].astype(BF16), whp_ref[...])
    m = gate_ref[:, :d] * ya + gate_ref[:, d:] * yb
    m = _dot(permt_ref[...], m.astype(BF16)).astype(BF16)
    mo = _dot(m, wo_ref[...]) + bo_ref[...]
    h0 = _layer_norm(x_ref[...].reshape(tm, d), lng_ref[...], lnb_ref[...])
    h = _layer_norm(DN_ALPHA * h0 + mo, g1_ref[...], b1_ref[...])
    hb = h.astype(BF16)
    acc = jnp.zeros(h.shape, F32)
    for c in range(0, w1_ref.shape[1], ffn_chunk):
        u = jnp.maximum(_dot(hb, w1_ref[:, c:c + ffn_chunk]) + bf1_ref[:, c:c + ffn_chunk], 0.0)
        acc = acc + _dot((u * u).astype(BF16), w2_ref[c:c + ffn_chunk, :])
    y = _layer_norm(DN_ALPHA * h + acc + bf2_ref[...], g2_ref[...], b2_ref[...])
    y_ref[...] = y.reshape(y_ref.shape)


def _merge_ffn(xv, permt, a, z, gate, consts, *, n1, n2):
    bsz, seq, d = z.shape
    j_per = TOKEN_TILE // n1
    ntile = n2 // j_per
    tok = lambda w: pl.BlockSpec((None, TOKEN_TILE, w), lambda b, t: (b, t, 0))
    nat = pl.BlockSpec((None, n1, j_per, d), lambda b, t: (b, 0, t, 0))
    a_prev = pl.BlockSpec((None, TOKEN_TILE, a.shape[2]), lambda b, t: (b, (t + ntile - 1) % ntile, 0))
    a_next = pl.BlockSpec((None, TOKEN_TILE, a.shape[2]), lambda b, t: (b, (t + 1) % ntile, 0))
    return pl.pallas_call(
        functools.partial(_merge_ffn_kernel, n1=n1, n2=n2, ffn_chunk=1024),
        grid=(bsz, ntile),
        in_specs=[nat, _const_spec(permt.shape), a_prev, tok(a.shape[2]), a_next, tok(d), tok(gate.shape[2]),
                  *[_const_spec(c.shape) for c in consts]],
        out_specs=nat,
        out_shape=jax.ShapeDtypeStruct(xv.shape, F32),
        compiler_params=_compiler_params(("parallel", "parallel")),
        name="merge_ffn",
    )(xv, permt, a, a, a, z, gate, *consts)


def _encoder(x, p):
    bsz, seq, d = x.shape
    n1, n2 = _split_factors(seq)
    pool_w = p['w_pool_proj'].shape[0]
    hy_w = p['w_hyena_proj'].shape[0]
    row = lambda v: v.reshape(1, -1).astype(F32)
    tabs = {k: jnp.asarray(v, BF16) for k, v in _dft_tables(n1, n2).items()}

    xv = x.reshape(bsz, n1, n2, d)
    perm = _row_permutation(n1, TOKEN_TILE // n1)
    a, u3, gate = _inproj(xv, jnp.asarray(perm, BF16), row(p['ln_in_g']), row(p['ln_in_b']),
                          p['w_in'].astype(BF16), row(p['b_in']),
                          n1=n1, n2=n2, widths=(pool_w, HYENA_ORDER * hy_w + hy_w, 2 * d))
    kf = _filter_spectra(p, tabs, n1=n1, n2=n2, width=hy_w)
    z = _hyena(u3, p['conv_w'], row(p['conv_b']), kf, p['hyena_bias'], tabs, n1=n1, n2=n2, width=hy_w)
    merge_consts = [row(p['ln_in_g']), row(p['ln_in_b']), p['w_pool'].astype(BF16), row(p['b_pool']),
                    row(p['pool_scale']), p['w_pool_proj'].astype(BF16), p['w_hyena_proj'].astype(BF16),
                    p['w_o'].astype(BF16), row(p['b_o']), row(p['ln1_g']), row(p['ln1_b']),
                    p['w_ff1'].astype(BF16), row(p['b_ff1']), p['w_ff2'].astype(BF16), row(p['b_ff2']),
                    row(p['ln2_g']), row(p['ln2_b'])]
    y = _merge_ffn(xv, jnp.asarray(perm.T, BF16), a, z, gate, merge_consts, n1=n1, n2=n2)
    return y.reshape(bsz, seq, d)


def kernel(x_prompt, x_sample, ln_in_g, ln_in_b, w_in, b_in, w_pool, b_pool, pool_scale, w_pool_proj, conv_w, conv_b, w_f1, b_f1, freq_f1, w_f2, b_f2, freq_f2, w_f_out, decay_rate, hyena_bias, w_hyena_proj, w_o, b_o, ln1_g, ln1_b, w_ff1, b_ff1, w_ff2, b_ff2, ln2_g, ln2_b):
    layer = dict(w_in=w_in, b_in=b_in, w_pool=w_pool, b_pool=b_pool, pool_scale=pool_scale,
                 w_pool_proj=w_pool_proj, conv_w=conv_w, conv_b=conv_b, w_f1=w_f1, b_f1=b_f1, freq_f1=freq_f1,
                 w_f2=w_f2, b_f2=b_f2, freq_f2=freq_f2, w_f_out=w_f_out, decay_rate=decay_rate,
                 hyena_bias=hyena_bias, w_hyena_proj=w_hyena_proj, w_o=w_o, b_o=b_o, ln1_g=ln1_g, ln1_b=ln1_b,
                 w_ff1=w_ff1, b_ff1=b_ff1, w_ff2=w_ff2, b_ff2=b_ff2, ln2_g=ln2_g, ln2_b=ln2_b)
    assert all(v.shape[0] == DEPTH for v in layer.values())
    p = {k: v[0] for k, v in layer.items()}
    p.update(ln_in_g=ln_in_g, ln_in_b=ln_in_b)
    y_prompt = _encoder(x_prompt, p)
    y_sample = _encoder(x_sample, p)
    return (y_prompt, y_sample)
```

```python
import functools
import math

import numpy as np
import jax
import jax.numpy as jnp
from jax import lax
from jax.experimental import pallas as pl
from jax.experimental.pallas import tpu as pltpu

F32 = jnp.float32
BF16 = jnp.bfloat16

V7X_LANES = 128
V7X_SUBLANES = 8
V7X_VMEM_BYTES = 64 * 1024 * 1024
VMEM_LIMIT_BYTES = V7X_VMEM_BYTES - 8 * 1024 * 1024

POOL_WINDOWS = (2, 4, 8, 16)
POOL_HALO = max(max(w // 2, w - 1 - w // 2) for w in POOL_WINDOWS)
N_DIRS = 2
HYENA_ORDER = 2
POS_BANDS = 16
LN_EPS = 1e-5
L1_EPS = 1e-6
DEPTH = 1
DN_ALPHA = (2.0 * DEPTH) ** 0.25

ROW_PAD = V7X_SUBLANES
TOKEN_TILE = 512


def _compiler_params(semantics):
    return pltpu.CompilerParams(dimension_semantics=semantics, vmem_limit_bytes=VMEM_LIMIT_BYTES)


def _const_spec(shape):
    nd = len(shape)
    return pl.BlockSpec(shape, lambda *_: (0,) * nd, pipeline_mode=pl.Buffered(1))


def _split_factors(seq_len):
    n2 = 64
    assert seq_len % n2 == 0
    return seq_len // n2, n2


def _dft_tables(n1, n2):
    seq = n1 * n2
    i1 = np.arange(n1)
    th = np.pi * np.outer(i1, i1) / n1
    f1 = np.zeros((2 * n1, n1))
    f1[:n1] = np.cos(th)
    f1[n1] = (-1.0) ** i1
    f1[n1 + 1:] = -np.sin(th[1:])
    g = np.zeros((n1, 2 * n1))
    g[:, :n1] = np.cos(th.T)
    g[:, n1] = (-1.0) ** i1
    g[:, n1 + 1:] = -np.sin(th.T[:, 1:])
    i2 = np.arange(n2)
    base = 2 * np.pi * np.outer(i2, i2) / n2
    m = np.zeros((n1, 2 * n2, 2 * n2))
    mi = np.zeros((n1, 2 * n2, 2 * n2))
    for k in range(1, n1):
        phi = base + 2 * np.pi * k * i2[None, :] / (2 * seq)
        c, s = np.cos(phi), np.sin(phi)
        m[k] = np.block([[c, s], [-s, c]])
        mi[k] = np.block([[c.T, -s.T], [s.T, c.T]]) / seq
    phin = base + np.pi * i2[None, :] / n2
    z = np.zeros((n2, n2))
    ms = np.block([[np.cos(base), z], [-np.sin(base), z], [z, np.cos(phin)], [z, -np.sin(phin)]])
    mis = np.block([[np.cos(base).T, -np.sin(base).T, z, z],
                    [z, z, np.cos(phin).T, -np.sin(phin).T]]) / (2 * seq)
    return dict(f1=f1, g=g, m=m, mi=mi, ms=ms, mis=mis)


def _perm_positions(n1, n2):
    return (np.arange(n2)[:, None] + n2 * np.arange(n1)[None, :]).reshape(-1)


def _positional_features(seq_len, n1, n2):
    pos = _perm_positions(n1, n2).astype(np.float64)
    t = pos / (seq_len - 1)
    w = 2.0 * math.pi * pos / seq_len
    f = np.linspace(1e-4, POS_BANDS - 1, POS_BANDS)
    z = np.concatenate([t[:, None], np.cos(f[None] * w[:, None]), -np.sin(f[None] * w[:, None])], axis=-1)
    zp = np.zeros((seq_len, 40))
    zp[:, :z.shape[1]] = z
    return zp, t[:, None]


def _layer_norm(x, g, b):
    mu = jnp.mean(x, axis=-1, keepdims=True)
    xc = x - mu
    var = jnp.mean(xc * xc, axis=-1, keepdims=True)
    return xc * lax.rsqrt(var + LN_EPS) * g + b


def _dot(a, b):
    return jnp.dot(a, b, preferred_element_type=F32)


def _split_bf16(x):
    hi = x.astype(BF16)
    return hi, (x - hi.astype(F32)).astype(BF16)


def _dot3(a, b):
    a_hi, a_lo = _split_bf16(a)
    b_hi, b_lo = _split_bf16(b)
    return _dot(a_hi, b_hi) + (_dot(a_hi, b_lo) + _dot(a_lo, b_hi))


def _store_slabs(dst_ref, row0, val):
    rows = val.shape[0]
    for j in range(dst_ref.shape[0]):
        dst_ref[j, pl.ds(row0, rows), :] = val[:, j * V7X_LANES:(j + 1) * V7X_LANES]


def _gather_pair(src_ref, start_a, start_b, count, pitch):
    cols = []
    for j in range(src_ref.shape[0]):
        a = src_ref[j, pl.ds(start_a, count, stride=pitch), :]
        b = src_ref[j, pl.ds(start_b, count, stride=pitch), :]
        cols.append(jnp.concatenate([a, b], axis=0))
    return cols[0] if len(cols) == 1 else jnp.concatenate(cols, axis=1)


def _complex_mul(x, k, half):
    xr, xi = x[:half], x[half:]
    kr, ki = k[:half], k[half:]
    return jnp.concatenate([xr * kr - xi * ki, xr * ki + xi * kr], axis=0)


def _row_permutation(n1, j_per):
    i, j = np.meshgrid(np.arange(n1), np.arange(j_per), indexing="ij")
    p = np.zeros((n1 * j_per, n1 * j_per), np.float32)
    p[(j * n1 + i).ravel(), (i * j_per + j).ravel()] = 1.0
    return p


def _inproj_kernel(x_ref, perm_ref, g_ref, b_ref, w_ref, bias_ref, a_ref, u_ref, gate_ref, h_scr, *, chunk):
    n1, j_per, d = x_ref.shape
    for i in range(0, n1, V7X_SUBLANES):
        x = x_ref[i:i + V7X_SUBLANES].reshape(V7X_SUBLANES * j_per, d)
        h_scr[i * j_per:(i + V7X_SUBLANES) * j_per, :] = _layer_norm(x, g_ref[...], b_ref[...]).astype(BF16)
    h = _dot(perm_ref[...], h_scr[...]).astype(BF16)
    wa, wu = a_ref.shape[1], u_ref.shape[1]
    total = w_ref.shape[1]
    for c in range(0, total, chunk):
        y = _dot(h, w_ref[:, c:c + chunk]) + bias_ref[:, c:c + chunk]
        if c < wa:
            a_ref[:, c:c + chunk] = y
        elif c < wa + wu:
            u_ref[:, c - wa:c - wa + chunk] = y
        else:
            gate_ref[:, c - wa - wu:c - wa - wu + chunk] = jax.nn.sigmoid(y)


def _inproj(xv, perm, ln_g, ln_b, w_in, b_in, *, n1, n2, widths):
    bsz = xv.shape[0]
    d = w_in.shape[0]
    seq = n1 * n2
    j_per = TOKEN_TILE // n1
    wa, wu, wg = widths
    chunk = 512
    assert wa % chunk == 0 and wu % chunk == 0 and wg % chunk == 0 and n2 % j_per == 0
    out_shapes = tuple(jax.ShapeDtypeStruct((bsz, seq, w), F32) for w in widths)
    tok = lambda w: pl.BlockSpec((None, TOKEN_TILE, w), lambda b, t: (b, t, 0))
    return pl.pallas_call(
        functools.partial(_inproj_kernel, chunk=chunk),
        grid=(bsz, n2 // j_per),
        in_specs=[
            pl.BlockSpec((None, n1, j_per, d), lambda b, t: (b, 0, t, 0)),
            _const_spec(perm.shape), _const_spec((1, d)), _const_spec((1, d)),
            _const_spec(w_in.shape), _const_spec(b_in.shape),
        ],
        out_specs=[tok(wa), tok(wu), tok(wg)],
        out_shape=out_shapes,
        scratch_shapes=[pltpu.VMEM((TOKEN_TILE, d), BF16)],
        compiler_params=_compiler_params(("parallel", "parallel")),
        name="inproj",
    )(xv, perm, ln_g, ln_b, w_in, b_in)


def _filter_kernel(z_ref, t_ref, w1_ref, b1_ref, q1_ref, w2_ref, b2_ref, q2_ref, wo_ref, dec_ref,
                   f1_ref, m_ref, ms_ref, kf_ref, hid_ref, h_ref, s1_ref, *, n1, n2):
    seq = n1 * n2
    pitch = 2 * n1 + ROW_PAD
    hp = lax.Precision.HIGHEST
    rb = 256

    def mlp(i, c):
        r = pl.multiple_of(i * rb, rb)
        z = z_ref[pl.ds(r, rb), :]
        h = jnp.sin(q1_ref[...] * (jnp.dot(z, w1_ref[...], precision=hp, preferred_element_type=F32) + b1_ref[...]))
        h = jnp.sin(q2_ref[...] * (jnp.dot(h, w2_ref[...], precision=hp, preferred_element_type=F32) + b2_ref[...]))
        hid_ref[pl.ds(r, rb), :] = h
        return c

    @pl.when((pl.program_id(0) == 0) & (pl.program_id(1) == 0))
    def _():
        lax.fori_loop(0, seq // rb, mlp, 0, unroll=4)

    sums = []
    for dr in range(N_DIRS):
        rate = jnp.abs(dec_ref[dr:dr + 1, :])

        def gen(i, acc, dr=dr, rate=rate):
            r = pl.multiple_of(i * rb, rb)
            h = _dot3(hid_ref[pl.ds(r, rb), :], wo_ref[dr])
            h = h * jnp.exp(-t_ref[pl.ds(r, rb), :] * rate)
            if dr == 1:
                row = lax.broadcasted_iota(jnp.int32, h.shape, 0) + r
                h = jnp.where(row == 0, 0.0, h)
            h_ref[dr, pl.ds(r, rb), :] = h
            return acc + jnp.sum(jnp.abs(h), axis=0, keepdims=True)

        sums.append(lax.fori_loop(0, seq // rb, gen, jnp.zeros((1, h_ref.shape[2]), F32), unroll=8))

    denom = sums[0] + sums[1] + L1_EPS
    for dr in range(N_DIRS):
        def stage1(i, c, dr=dr):
            r = pl.multiple_of(i * n1, n1)
            blk = h_ref[dr, pl.ds(r, n1), :] / denom
            _store_slabs(s1_ref, pl.multiple_of(i * pitch, V7X_SUBLANES), _dot(f1_ref[...], blk.astype(BF16)))
            return c

        lax.fori_loop(0, n2, stage1, 0, unroll=16)

        def emit(row0, x, half, dr=dr):
            if dr == 0:
                kf_ref[pl.ds(row0, 2 * half), :] = x
            else:
                sign = jnp.where(lax.broadcasted_iota(jnp.int32, x.shape, 0) < half, 1.0, -1.0)
                kf_ref[pl.ds(row0, 2 * half), :] += sign * x

        b0 = _gather_pair(s1_ref, 0, n1, n2, pitch).astype(BF16)
        x0 = _dot(ms_ref[...], b0)
        emit(0, x0[:2 * n2], n2)
        emit(2 * n2, x0[2 * n2:], n2)

        def stage2(k, c, emit=emit):
            bk = _gather_pair(s1_ref, k, n1 + k, n2, pitch).astype(BF16)
            emit(pl.multiple_of((k + 1) * 2 * n2, 2 * n2), _dot(m_ref[k], bk), n2)
            return c

        lax.fori_loop(1, n1, stage2, 0, unroll=21)


def _filter_spectra(p, tabs, *, n1, n2, width):
    seq = n1 * n2
    cb = 2 * V7X_LANES
    nrow = 2 * n2 * (n1 + 1)
    pitch = 2 * n1 + ROW_PAD
    zfeat, tcol = _positional_features(seq, n1, n2)
    hid = p['w_f2'].shape[0]
    w1 = jnp.zeros((zfeat.shape[1], hid), F32).at[:p['w_f1'].shape[0]].set(p['w_f1'])
    wo = p['w_f_out'].reshape(hid, HYENA_ORDER, N_DIRS, width).transpose(1, 2, 0, 3)
    row = lambda v: v.reshape(1, -1)
    operands = [jnp.asarray(zfeat, F32), jnp.asarray(tcol, F32), w1, row(p['b_f1']), row(p['freq_f1']),
                p['w_f2'], row(p['b_f2']), row(p['freq_f2']), wo,
                p['decay_rate'].reshape(HYENA_ORDER, N_DIRS, width),
                tabs['f1'], tabs['m'], tabs['ms']]
    in_specs = [_const_spec(o.shape) for o in operands]
    in_specs[8] = pl.BlockSpec((None, N_DIRS, hid, cb), lambda c, o: (o, 0, 0, c))
    in_specs[9] = pl.BlockSpec((None, N_DIRS, cb), lambda c, o: (o, 0, c))
    return pl.pallas_call(
        functools.partial(_filter_kernel, n1=n1, n2=n2),
        grid=(width // cb, HYENA_ORDER),
        in_specs=in_specs,
        out_specs=pl.BlockSpec((None, nrow, cb), lambda c, o: (o, 0, c)),
        out_shape=jax.ShapeDtypeStruct((HYENA_ORDER, nrow, width), F32),
        scratch_shapes=[pltpu.VMEM((seq, hid), F32), pltpu.VMEM((N_DIRS, seq, cb), F32),
                        pltpu.VMEM((cb // V7X_LANES, n2 * pitch, V7X_LANES), F32)],
        compiler_params=_compiler_params(("arbitrary", "arbitrary")),
        name="filter_spectra",
    )(*operands)


def _hyena_kernel(x1_ref, x2_ref, v_ref, cw1_ref, cw2_ref, cwv_ref, cb1_ref, cb2_ref, cbv_ref,
                  kf_ref, hb_ref, f1_ref, g_ref, m_ref, mi_ref, ms_ref, mis_ref,
                  z_ref, u_scr, w_scr, s1_ref, s2_ref, *, n1, n2):
    pitch1 = 2 * n1 + ROW_PAD
    pitch2 = 2 * n2 + ROW_PAD
    cbw = z_ref.shape[1]
    rows = lax.broadcasted_iota(jnp.int32, (n1, cbw), 0)

    def short_conv(src, w_ref, b_ref, i):
        r = i * n1
        if i == 0:
            prev = jnp.where(rows == 0, 0.0, src[pl.ds((n2 - 1) * n1 - 1, n1), :])
        else:
            prev = src[pl.ds(r - n1, n1), :]
        if i == n2 - 1:
            nxt = jnp.where(rows == n1 - 1, 0.0, src[pl.ds(1, n1), :])
        else:
            nxt = src[pl.ds(r + n1, n1), :]
        return prev * w_ref[0:1, :] + src[pl.ds(r, n1), :] * w_ref[1:2, :] + nxt * w_ref[2:3, :] + b_ref[...]

    def stage1(i, u):
        _store_slabs(s1_ref, i * pitch1, _dot(f1_ref[...], u.astype(BF16)))

    def middle(o):
        b0 = _gather_pair(s1_ref, 0, n1, n2, pitch1).astype(BF16)
        x0 = _dot(ms_ref[...], b0)
        y0 = jnp.concatenate([_complex_mul(x0[:2 * n2], kf_ref[o, 0:2 * n2, :], n2),
                              _complex_mul(x0[2 * n2:], kf_ref[o, 2 * n2:4 * n2, :], n2)], axis=0)
        _store_slabs(s2_ref, 0, _dot(mis_ref[...], y0.astype(BF16)))
        for k in range(1, n1):
            bk = _gather_pair(s1_ref, k, n1 + k, n2, pitch1).astype(BF16)
            xk = _dot(m_ref[k], bk)
            yk = _complex_mul(xk, kf_ref[o, (k + 1) * 2 * n2:(k + 2) * 2 * n2, :], n2).astype(BF16)
            _store_slabs(s2_ref, k * pitch2, _dot(mi_ref[k], yk))

    def finish(o, gate_ref, gw_ref, gb_ref, u_in, z_out):
        for i in range(n2):
            dk = _gather_pair(s2_ref, i, n2 + i, n1, pitch2).astype(BF16)
            conv = _dot(g_ref[...], dk) + u_in[pl.ds(i * n1, n1), :] * hb_ref[o:o + 1, :]
            z_out[pl.ds(i * n1, n1), :] = short_conv(gate_ref, gw_ref, gb_ref, i) * conv

    for i in range(n2):
        u = short_conv(v_ref, cwv_ref, cbv_ref, i)
        u_scr[pl.ds(i * n1, n1), :] = u
        stage1(i, u)
    middle(0)
    finish(0, x1_ref, cw1_ref, cb1_ref, u_scr, w_scr)
    for i in range(n2):
        stage1(i, w_scr[pl.ds(i * n1, n1), :])
    middle(1)
    finish(1, x2_ref, cw2_ref, cb2_ref, w_scr, z_ref)


def _hyena_block_width(n1, n2):
    seq = n1 * n2
    nrow = 2 * n2 * (n1 + 1)

    def vmem_bytes(cb):
        windows = 2 * 4 * seq * cb
        scratch = 2 * seq * cb + n2 * (2 * n1 + ROW_PAD) * cb + n1 * (2 * n2 + ROW_PAD) * cb
        tables = 2 * (2 * n1 * (2 * n2) ** 2 + 2 * 4 * n2 * 2 * n2 + 2 * 2 * n1 * n1) // 4
        return 4 * (windows + scratch + HYENA_ORDER * nrow * cb + tables)

    return next(c for c in (2 * V7X_LANES, V7X_LANES) if vmem_bytes(c) <= VMEM_LIMIT_BYTES)


def _hyena(u3, conv_w, conv_b, kf, hyena_bias, tabs, *, n1, n2, width):
    cb = _hyena_block_width(n1, n2)
    bsz, seq, _ = u3.shape
    ncb = width // cb
    nrow = kf.shape[1]
    pitch1 = 2 * n1 + ROW_PAD
    pitch2 = 2 * n2 + ROW_PAD
    nslab = cb // V7X_LANES
    seq_spec = lambda part: pl.BlockSpec((None, seq, cb), lambda c, b: (b, 0, part * ncb + c))
    tap_spec = lambda part: pl.BlockSpec((conv_w.shape[0], cb), lambda c, b: (0, part * ncb + c))
    bias_spec = lambda part: pl.BlockSpec((1, cb), lambda c, b: (0, part * ncb + c))
    consts = [tabs[k] for k in ('f1', 'g', 'm', 'mi', 'ms', 'mis')]
    return pl.pallas_call(
        functools.partial(_hyena_kernel, n1=n1, n2=n2),
        grid=(ncb, bsz),
        in_specs=[seq_spec(0), seq_spec(1), seq_spec(2),
                  tap_spec(0), tap_spec(1), tap_spec(2), bias_spec(0), bias_spec(1), bias_spec(2),
                  pl.BlockSpec((HYENA_ORDER, nrow, cb), lambda c, b: (0, 0, c), pipeline_mode=pl.Buffered(1)),
                  pl.BlockSpec((HYENA_ORDER, cb), lambda c, b: (0, c)),
                  *[_const_spec(t.shape) for t in consts]],
        out_specs=pl.BlockSpec((None, seq, cb), lambda c, b: (b, 0, c)),
        out_shape=jax.ShapeDtypeStruct((bsz, seq, width), F32),
        scratch_shapes=[pltpu.VMEM((seq, cb), F32), pltpu.VMEM((seq, cb), F32),
                        pltpu.VMEM((nslab, n2 * pitch1, V7X_LANES), F32),
                        pltpu.VMEM((nslab, n1 * pitch2, V7X_LANES), F32)],
        compiler_params=_compiler_params(("parallel", "parallel")),
        name="hyena",
    )(u3, u3, u3, conv_w, conv_w, conv_w, conv_b, conv_b, conv_b, kf, hyena_bias, *consts)


def _pool_tile(prev_ref, cur_ref, next_ref, *, n1, n2):
    j_per = cur_ref.shape[0] // n1
    assert POOL_HALO <= j_per
    seq = n1 * n2
    t = pl.program_id(1)
    first, last = t == 0, t == pl.num_programs(1) - 1
    rows = lax.broadcasted_iota(jnp.int32, (n1, V7X_LANES), 0)
    cols = []
    for g, win in enumerate(POOL_WINDOWS):
        lo = win // 2
        hi = win - 1 - lo
        lanes = slice(g * V7X_LANES, (g + 1) * V7X_LANES)
        memo = {}

        def blk(m):
            if m not in memo:
                if m < 0:
                    b = prev_ref[(j_per + m) * n1:(j_per + m + 1) * n1, lanes]
                    memo[m] = jnp.where(first, jnp.where(rows == 0, 0.0, pltpu.roll(b, 1, 0)), b)
                elif m >= j_per:
                    b = next_ref[(m - j_per) * n1:(m - j_per + 1) * n1, lanes]
                    memo[m] = jnp.where(last, jnp.where(rows == n1 - 1, 0.0, pltpu.roll(b, n1 - 1, 0)), b)
                else:
                    memo[m] = cur_ref[m * n1:(m + 1) * n1, lanes]
            return memo[m]

        outs = []
        for j in range(j_per):
            acc = blk(j - lo)
            for dlt in range(-lo + 1, hi + 1):
                acc = acc + blk(j + dlt)
            pos = rows * n2 + (t * j_per + j)
            cnt = jnp.minimum(pos + (win - lo), seq) - jnp.maximum(pos - lo, 0)
            outs.append(acc / cnt.astype(F32) - blk(j))
        cols.append(jnp.concatenate(outs, axis=0))
    return jnp.concatenate(cols, axis=1)


def _merge_ffn_kernel(x_ref, permt_ref, ap_ref, ac_ref, an_ref, z_ref, gate_ref, lng_ref, lnb_ref, wp_ref, bp_ref,
                      ps_ref, wpp_ref, whp_ref, wo_ref, bo_ref, g1_ref, b1_ref, w1_ref, bf1_ref, w2_ref, bf2_ref,
                      g2_ref, b2_ref, y_ref, *, n1, n2, ffn_chunk):
    tm, d = z_ref.shape
    ngrp = wp_ref.shape[0]
    gd = wp_ref.shape[1]
    pm = _pool_tile(ap_ref, ac_ref, an_ref, n1=n1, n2=n2).astype(BF16)
    p = jnp.concatenate([_dot(pm[:, g * gd:(g + 1) * gd], wp_ref[g]) for g in range(ngrp)], axis=1)
    p = (p + bp_ref[...]) * ps_ref[...]
    ya = _dot(p.astype(BF16), wpp_ref[...])
    yb = _dot(z_ref[...].astype(BF16), whp_ref[...])
    m = gate_ref[:, :d] * ya + gate_ref[:, d:] * yb
    m = _dot(permt_ref[...], m.astype(BF16)).astype(BF16)
    mo = _dot(m, wo_ref[...]) + bo_ref[...]
    h0 = _layer_norm(x_ref[...].reshape(tm, d), lng_ref[...], lnb_ref[...])
    h = _layer_norm(DN_ALPHA * h0 + mo, g1_ref[...], b1_ref[...])
    hb = h.astype(BF16)
    acc = jnp.zeros(h.shape, F32)
    for c in range(0, w1_ref.shape[1], ffn_chunk):
        u = jnp.maximum(_dot(hb, w1_ref[:, c:c + ffn_chunk]) + bf1_ref[:, c:c + ffn_chunk], 0.0)
        acc = acc + _dot((u * u).astype(BF16), w2_ref[c:c + ffn_chunk, :])
    y = _layer_norm(DN_ALPHA * h + acc + bf2_ref[...], g2_ref[...], b2_ref[...])
    y_ref[...] = y.reshape(y_ref.shape)


def _merge_ffn(xv, permt, a, z, gate, consts, *, n1, n2):
    bsz, seq, d = z.shape
    j_per = TOKEN_TILE // n1
    ntile = n2 // j_per
    tok = lambda w: pl.BlockSpec((None, TOKEN_TILE, w), lambda b, t: (b, t, 0))
    nat = pl.BlockSpec((None, n1, j_per, d), lambda b, t: (b, 0, t, 0))
    a_prev = pl.BlockSpec((None, TOKEN_TILE, a.shape[2]), lambda b, t: (b, (t + ntile - 1) % ntile, 0))
    a_next = pl.BlockSpec((None, TOKEN_TILE, a.shape[2]), lambda b, t: (b, (t + 1) % ntile, 0))
    return pl.pallas_call(
        functools.partial(_merge_ffn_kernel, n1=n1, n2=n2, ffn_chunk=2048),
        grid=(bsz, ntile),
        in_specs=[nat, _const_spec(permt.shape), a_prev, tok(a.shape[2]), a_next, tok(d), tok(gate.shape[2]),
                  *[_const_spec(c.shape) for c in consts]],
        out_specs=nat,
        out_shape=jax.ShapeDtypeStruct(xv.shape, F32),
        compiler_params=_compiler_params(("parallel", "parallel")),
        name="merge_ffn",
    )(xv, permt, a, a, a, z, gate, *consts)


def _encoder(x, p):
    bsz, seq, d = x.shape
    n1, n2 = _split_factors(seq)
    pool_w = p['w_pool_proj'].shape[0]
    hy_w = p['w_hyena_proj'].shape[0]
    row = lambda v: v.reshape(1, -1).astype(F32)
    tabs = {k: jnp.asarray(v, BF16) for k, v in _dft_tables(n1, n2).items()}

    xv = x.reshape(bsz, n1, n2, d)
    perm = _row_permutation(n1, TOKEN_TILE // n1)
    a, u3, gate = _inproj(xv, jnp.asarray(perm, BF16), row(p['ln_in_g']), row(p['ln_in_b']),
                          p['w_in'].astype(BF16), row(p['b_in']),
                          n1=n1, n2=n2, widths=(pool_w, HYENA_ORDER * hy_w + hy_w, 2 * d))
    kf = _filter_spectra(p, tabs, n1=n1, n2=n2, width=hy_w)
    z = _hyena(u3, p['conv_w'], row(p['conv_b']), kf, p['hyena_bias'], tabs, n1=n1, n2=n2, width=hy_w)
    merge_consts = [row(p['ln_in_g']), row(p['ln_in_b']), p['w_pool'].astype(BF16), row(p['b_pool']),
                    row(p['pool_scale']), p['w_pool_proj'].astype(BF16), p['w_hyena_proj'].astype(BF16),
                    p['w_o'].astype(BF16), row(p['b_o']), row(p['ln1_g']), row(p['ln1_b']),
                    p['w_ff1'].astype(BF16), row(p['b_ff1']), p['w_ff2'].astype(BF16), row(p['b_ff2']),
                    row(p['ln2_g']), row(p['ln2_b'])]
    y = _merge_ffn(xv, jnp.asarray(perm.T, BF16), a, z, gate, merge_consts, n1=n1, n2=n2)
    return y.reshape(bsz, seq, d)


def kernel(x_prompt, x_sample, ln_in_g, ln_in_b, w_in, b_in, w_pool, b_pool, pool_scale, w_pool_proj, conv_w, conv_b, w_f1, b_f1, freq_f1, w_f2, b_f2, freq_f2, w_f_out, decay_rate, hyena_bias, w_hyena_proj, w_o, b_o, ln1_g, ln1_b, w_ff1, b_ff1, w_ff2, b_ff2, ln2_g, ln2_b):
    layer = dict(w_in=w_in, b_in=b_in, w_pool=w_pool, b_pool=b_pool, pool_scale=pool_scale,
                 w_pool_proj=w_pool_proj, conv_w=conv_w, conv_b=conv_b, w_f1=w_f1, b_f1=b_f1, freq_f1=freq_f1,
                 w_f2=w_f2, b_f2=b_f2, freq_f2=freq_f2, w_f_out=w_f_out, decay_rate=decay_rate,
                 hyena_bias=hyena_bias, w_hyena_proj=w_hyena_proj, w_o=w_o, b_o=b_o, ln1_g=ln1_g, ln1_b=ln1_b,
                 w_ff1=w_ff1, b_ff1=b_ff1, w_ff2=w_ff2, b_ff2=b_ff2, ln2_g=ln2_g, ln2_b=ln2_b)
    assert all(v.shape[0] == DEPTH for v in layer.values())
    p = {k: v[0] for k, v in layer.items()}
    p.update(ln_in_g=ln_in_g, ln_in_b=ln_in_b)
    y_prompt = _encoder(x_prompt, p)
    y_sample = _encoder(x_sample, p)
    return (y_prompt, y_sample)
```

```python
import functools
import math

import numpy as np
import jax
import jax.numpy as jnp
from jax import lax
from jax.experimental import pallas as pl
from jax.experimental.pallas import tpu as pltpu

F32 = jnp.float32
BF16 = jnp.bfloat16

V7X_LANES = 128
V7X_SUBLANES = 8
V7X_VMEM_BYTES = 64 * 1024 * 1024
VMEM_LIMIT_BYTES = V7X_VMEM_BYTES - 8 * 1024 * 1024

POOL_WINDOWS = (2, 4, 8, 16)
POOL_HALO = max(max(w // 2, w - 1 - w // 2) for w in POOL_WINDOWS)
N_DIRS = 2
HYENA_ORDER = 2
POS_BANDS = 16
LN_EPS = 1e-5
L1_EPS = 1e-6
DEPTH = 1
DN_ALPHA = (2.0 * DEPTH) ** 0.25

ROW_PAD = V7X_SUBLANES
TOKEN_TILE = 512


def _compiler_params(semantics):
    return pltpu.CompilerParams(dimension_semantics=semantics, vmem_limit_bytes=VMEM_LIMIT_BYTES)


def _const_spec(shape):
    nd = len(shape)
    return pl.BlockSpec(shape, lambda *_: (0,) * nd, pipeline_mode=pl.Buffered(1))


def _split_factors(seq_len):
    n2 = 64
    assert seq_len % n2 == 0
    return seq_len // n2, n2


def _dft_tables(n1, n2):
    seq = n1 * n2
    i1 = np.arange(n1)
    th = np.pi * np.outer(i1, i1) / n1
    f1 = np.zeros((2 * n1, n1))
    f1[:n1] = np.cos(th)
    f1[n1] = (-1.0) ** i1
    f1[n1 + 1:] = -np.sin(th[1:])
    g = np.zeros((n1, 2 * n1))
    g[:, :n1] = np.cos(th.T)
    g[:, n1] = (-1.0) ** i1
    g[:, n1 + 1:] = -np.sin(th.T[:, 1:])
    i2 = np.arange(n2)
    base = 2 * np.pi * np.outer(i2, i2) / n2
    m = np.zeros((n1, 2 * n2, 2 * n2))
    mi = np.zeros((n1, 2 * n2, 2 * n2))
    for k in range(1, n1):
        phi = base + 2 * np.pi * k * i2[None, :] / (2 * seq)
        c, s = np.cos(phi), np.sin(phi)
        m[k] = np.block([[c, s], [-s, c]])
        mi[k] = np.block([[c.T, -s.T], [s.T, c.T]]) / seq
    phin = base + np.pi * i2[None, :] / n2
    z = np.zeros((n2, n2))
    ms = np.block([[np.cos(base), z], [-np.sin(base), z], [z, np.cos(phin)], [z, -np.sin(phin)]])
    mis = np.block([[np.cos(base).T, -np.sin(base).T, z, z],
                    [z, z, np.cos(phin).T, -np.sin(phin).T]]) / (2 * seq)
    return dict(f1=f1, g=g, m=m, mi=mi, ms=ms, mis=mis)


def _perm_positions(n1, n2):
    return (np.arange(n2)[:, None] + n2 * np.arange(n1)[None, :]).reshape(-1)


def _positional_features(seq_len, n1, n2):
    pos = _perm_positions(n1, n2).astype(np.float64)
    t = pos / (seq_len - 1)
    w = 2.0 * math.pi * pos / seq_len
    f = np.linspace(1e-4, POS_BANDS - 1, POS_BANDS)
    z = np.concatenate([t[:, None], np.cos(f[None] * w[:, None]), -np.sin(f[None] * w[:, None])], axis=-1)
    zp = np.zeros((seq_len, 40))
    zp[:, :z.shape[1]] = z
    return zp, t[:, None]


def _layer_norm(x, g, b):
    mu = jnp.mean(x, axis=-1, keepdims=True)
    xc = x - mu
    var = jnp.mean(xc * xc, axis=-1, keepdims=True)
    return xc * lax.rsqrt(var + LN_EPS) * g + b


def _dot(a, b):
    return jnp.dot(a, b, preferred_element_type=F32)


def _split_bf16(x):
    hi = x.astype(BF16)
    return hi, (x - hi.astype(F32)).astype(BF16)


def _dot3(a, b):
    a_hi, a_lo = _split_bf16(a)
    b_hi, b_lo = _split_bf16(b)
    return _dot(a_hi, b_hi) + (_dot(a_hi, b_lo) + _dot(a_lo, b_hi))


def _store_slabs(dst_ref, row0, val):
    rows = val.shape[0]
    for j in range(dst_ref.shape[0]):
        dst_ref[j, pl.ds(row0, rows), :] = val[:, j * V7X_LANES:(j + 1) * V7X_LANES]


def _gather_pair(src_ref, start_a, start_b, count, pitch):
    cols = []
    for j in range(src_ref.shape[0]):
        a = src_ref[j, pl.ds(start_a, count, stride=pitch), :]
        b = src_ref[j, pl.ds(start_b, count, stride=pitch), :]
        cols.append(jnp.concatenate([a, b], axis=0))
    return cols[0] if len(cols) == 1 else jnp.concatenate(cols, axis=1)


def _complex_mul(x, k, half):
    xr, xi = x[:half], x[half:]
    kr, ki = k[:half], k[half:]
    return jnp.concatenate([xr * kr - xi * ki, xr * ki + xi * kr], axis=0)


def _row_permutation(n1, j_per):
    i, j = np.meshgrid(np.arange(n1), np.arange(j_per), indexing="ij")
    p = np.zeros((n1 * j_per, n1 * j_per), np.float32)
    p[(j * n1 + i).ravel(), (i * j_per + j).ravel()] = 1.0
    return p


def _inproj_kernel(x_ref, perm_ref, g_ref, b_ref, w_ref, bias_ref, a_ref, u_ref, gate_ref, h_scr, *, chunk):
    n1, j_per, d = x_ref.shape
    for i in range(0, n1, V7X_SUBLANES):
        x = x_ref[i:i + V7X_SUBLANES].reshape(V7X_SUBLANES * j_per, d)
        h_scr[i * j_per:(i + V7X_SUBLANES) * j_per, :] = _layer_norm(x, g_ref[...], b_ref[...]).astype(BF16)
    h = _dot(perm_ref[...], h_scr[...]).astype(BF16)
    wa, wu = a_ref.shape[1], u_ref.shape[1]
    total = w_ref.shape[1]
    for c in range(0, total, chunk):
        y = _dot(h, w_ref[:, c:c + chunk]) + bias_ref[:, c:c + chunk]
        if c < wa:
            a_ref[:, c:c + chunk] = y
        elif c < wa + wu:
            u_ref[:, c - wa:c - wa + chunk] = y
        else:
            gate_ref[:, c - wa - wu:c - wa - wu + chunk] = jax.nn.sigmoid(y)


def _inproj(xv, perm, ln_g, ln_b, w_in, b_in, *, n1, n2, widths):
    bsz = xv.shape[0]
    d = w_in.shape[0]
    seq = n1 * n2
    j_per = TOKEN_TILE // n1
    wa, wu, wg = widths
    chunk = 512
    assert wa % chunk == 0 and wu % chunk == 0 and wg % chunk == 0 and n2 % j_per == 0
    out_shapes = tuple(jax.ShapeDtypeStruct((bsz, seq, w), F32) for w in widths)
    tok = lambda w: pl.BlockSpec((None, TOKEN_TILE, w), lambda b, t: (b, t, 0))
    return pl.pallas_call(
        functools.partial(_inproj_kernel, chunk=chunk),
        grid=(bsz, n2 // j_per),
        in_specs=[
            pl.BlockSpec((None, n1, j_per, d), lambda b, t: (b, 0, t, 0)),
            _const_spec(perm.shape), _const_spec((1, d)), _const_spec((1, d)),
            _const_spec(w_in.shape), _const_spec(b_in.shape),
        ],
        out_specs=[tok(wa), tok(wu), tok(wg)],
        out_shape=out_shapes,
        scratch_shapes=[pltpu.VMEM((TOKEN_TILE, d), BF16)],
        compiler_params=_compiler_params(("parallel", "parallel")),
        name="inproj",
    )(xv, perm, ln_g, ln_b, w_in, b_in)


def _filter_kernel(z_ref, t_ref, w1_ref, b1_ref, q1_ref, w2_ref, b2_ref, q2_ref, wo_ref, dec_ref,
                   f1_ref, m_ref, ms_ref, kf_ref, hid_ref, h_ref, s1_ref, *, n1, n2):
    seq = n1 * n2
    pitch = 2 * n1 + ROW_PAD
    hp = lax.Precision.HIGHEST
    rb = 256

    def mlp(i, c):
        r = pl.multiple_of(i * rb, rb)
        z = z_ref[pl.ds(r, rb), :]
        h = jnp.sin(q1_ref[...] * (jnp.dot(z, w1_ref[...], precision=hp, preferred_element_type=F32) + b1_ref[...]))
        h = jnp.sin(q2_ref[...] * (jnp.dot(h, w2_ref[...], precision=hp, preferred_element_type=F32) + b2_ref[...]))
        hid_ref[pl.ds(r, rb), :] = h
        return c

    @pl.when((pl.program_id(0) == 0) & (pl.program_id(1) == 0))
    def _():
        lax.fori_loop(0, seq // rb, mlp, 0, unroll=4)

    sums = []
    for dr in range(N_DIRS):
        rate = jnp.abs(dec_ref[dr:dr + 1, :])

        def gen(i, acc, dr=dr, rate=rate):
            r = pl.multiple_of(i * rb, rb)
            h = _dot3(hid_ref[pl.ds(r, rb), :], wo_ref[dr])
            h = h * jnp.exp(-t_ref[pl.ds(r, rb), :] * rate)
            if dr == 1:
                row = lax.broadcasted_iota(jnp.int32, h.shape, 0) + r
                h = jnp.where(row == 0, 0.0, h)
            h_ref[dr, pl.ds(r, rb), :] = h
            return acc + jnp.sum(jnp.abs(h), axis=0, keepdims=True)

        sums.append(lax.fori_loop(0, seq // rb, gen, jnp.zeros((1, h_ref.shape[2]), F32), unroll=8))

    denom = sums[0] + sums[1] + L1_EPS
    for dr in range(N_DIRS):
        def stage1(i, c, dr=dr):
            r = pl.multiple_of(i * n1, n1)
            blk = h_ref[dr, pl.ds(r, n1), :] / denom
            _store_slabs(s1_ref, pl.multiple_of(i * pitch, V7X_SUBLANES), _dot(f1_ref[...], blk.astype(BF16)))
            return c

        lax.fori_loop(0, n2, stage1, 0, unroll=16)

        def emit(row0, x, half, dr=dr):
            if dr == 0:
                kf_ref[pl.ds(row0, 2 * half), :] = x
            else:
                sign = jnp.where(lax.broadcasted_iota(jnp.int32, x.shape, 0) < half, 1.0, -1.0)
                kf_ref[pl.ds(row0, 2 * half), :] += sign * x

        b0 = _gather_pair(s1_ref, 0, n1, n2, pitch).astype(BF16)
        x0 = _dot(ms_ref[...], b0)
        emit(0, x0[:2 * n2], n2)
        emit(2 * n2, x0[2 * n2:], n2)

        def stage2(k, c, emit=emit):
            bk = _gather_pair(s1_ref, k, n1 + k, n2, pitch).astype(BF16)
            emit(pl.multiple_of((k + 1) * 2 * n2, 2 * n2), _dot(m_ref[k], bk), n2)
            return c

        lax.fori_loop(1, n1, stage2, 0, unroll=21)


def _filter_spectra(p, tabs, *, n1, n2, width):
    seq = n1 * n2
    cb = 2 * V7X_LANES
    nrow = 2 * n2 * (n1 + 1)
    pitch = 2 * n1 + ROW_PAD
    zfeat, tcol = _positional_features(seq, n1, n2)
    hid = p['w_f2'].shape[0]
    w1 = jnp.zeros((zfeat.shape[1], hid), F32).at[:p['w_f1'].shape[0]].set(p['w_f1'])
    wo = p['w_f_out'].reshape(hid, HYENA_ORDER, N_DIRS, width).transpose(1, 2, 0, 3)
    row = lambda v: v.reshape(1, -1)
    operands = [jnp.asarray(zfeat, F32), jnp.asarray(tcol, F32), w1, row(p['b_f1']), row(p['freq_f1']),
                p['w_f2'], row(p['b_f2']), row(p['freq_f2']), wo,
                p['decay_rate'].reshape(HYENA_ORDER, N_DIRS, width),
                tabs['f1'], tabs['m'], tabs['ms']]
    in_specs = [_const_spec(o.shape) for o in operands]
    in_specs[8] = pl.BlockSpec((None, N_DIRS, hid, cb), lambda c, o: (o, 0, 0, c))
    in_specs[9] = pl.BlockSpec((None, N_DIRS, cb), lambda c, o: (o, 0, c))
    return pl.pallas_call(
        functools.partial(_filter_kernel, n1=n1, n2=n2),
        grid=(width // cb, HYENA_ORDER),
        in_specs=in_specs,
        out_specs=pl.BlockSpec((None, nrow, cb), lambda c, o: (o, 0, c)),
        out_shape=jax.ShapeDtypeStruct((HYENA_ORDER, nrow, width), F32),
        scratch_shapes=[pltpu.VMEM((seq, hid), F32), pltpu.VMEM((N_DIRS, seq, cb), F32),
                        pltpu.VMEM((cb // V7X_LANES, n2 * pitch, V7X_LANES), F32)],
        compiler_params=_compiler_params(("arbitrary", "arbitrary")),
        name="filter_spectra",
    )(*operands)


def _hyena_kernel(x1_ref, x2_ref, v_ref, cw1_ref, cw2_ref, cwv_ref, cb1_ref, cb2_ref, cbv_ref,
                  kf_ref, hb_ref, f1_ref, g_ref, m_ref, mi_ref, ms_ref, mis_ref,
                  z_ref, u_scr, w_scr, s1_ref, s2_ref, *, n1, n2):
    pitch1 = 2 * n1 + ROW_PAD
    pitch2 = 2 * n2 + ROW_PAD
    cbw = z_ref.shape[1]
    rows = lax.broadcasted_iota(jnp.int32, (n1, cbw), 0)

    def short_conv(src, w_ref, b_ref, i):
        r = i * n1
        if i == 0:
            prev = jnp.where(rows == 0, 0.0, src[pl.ds((n2 - 1) * n1 - 1, n1), :])
        else:
            prev = src[pl.ds(r - n1, n1), :]
        if i == n2 - 1:
            nxt = jnp.where(rows == n1 - 1, 0.0, src[pl.ds(1, n1), :])
        else:
            nxt = src[pl.ds(r + n1, n1), :]
        return prev * w_ref[0:1, :] + src[pl.ds(r, n1), :] * w_ref[1:2, :] + nxt * w_ref[2:3, :] + b_ref[...]

    def stage1(i, u):
        _store_slabs(s1_ref, i * pitch1, _dot(f1_ref[...], u.astype(BF16)))

    def middle(o):
        b0 = _gather_pair(s1_ref, 0, n1, n2, pitch1).astype(BF16)
        x0 = _dot(ms_ref[...], b0)
        y0 = jnp.concatenate([_complex_mul(x0[:2 * n2], kf_ref[o, 0:2 * n2, :], n2),
                              _complex_mul(x0[2 * n2:], kf_ref[o, 2 * n2:4 * n2, :], n2)], axis=0)
        _store_slabs(s2_ref, 0, _dot(mis_ref[...], y0.astype(BF16)))
        for k in range(1, n1):
            bk = _gather_pair(s1_ref, k, n1 + k, n2, pitch1).astype(BF16)
            xk = _dot(m_ref[k], bk)
            yk = _complex_mul(xk, kf_ref[o, (k + 1) * 2 * n2:(k + 2) * 2 * n2, :], n2).astype(BF16)
            _store_slabs(s2_ref, k * pitch2, _dot(mi_ref[k], yk))

    def finish(o, gate_ref, gw_ref, gb_ref, u_in, z_out):
        for i in range(n2):
            dk = _gather_pair(s2_ref, i, n2 + i, n1, pitch2).astype(BF16)
            conv = _dot(g_ref[...], dk) + u_in[pl.ds(i * n1, n1), :] * hb_ref[o:o + 1, :]
            z_out[pl.ds(i * n1, n1), :] = short_conv(gate_ref, gw_ref, gb_ref, i) * conv

    for i in range(n2):
        u = short_conv(v_ref, cwv_ref, cbv_ref, i)
        u_scr[pl.ds(i * n1, n1), :] = u
        stage1(i, u)
    middle(0)
    finish(0, x1_ref, cw1_ref, cb1_ref, u_scr, w_scr)
    for i in range(n2):
        stage1(i, w_scr[pl.ds(i * n1, n1), :])
    middle(1)
    finish(1, x2_ref, cw2_ref, cb2_ref, w_scr, z_ref)


def _hyena_block_width(n1, n2):
    seq = n1 * n2
    nrow = 2 * n2 * (n1 + 1)

    def vmem_bytes(cb):
        windows = 2 * 4 * seq * cb
        scratch = 2 * seq * cb + n2 * (2 * n1 + ROW_PAD) * cb + n1 * (2 * n2 + ROW_PAD) * cb
        tables = 2 * (2 * n1 * (2 * n2) ** 2 + 2 * 4 * n2 * 2 * n2 + 2 * 2 * n1 * n1) // 4
        spectra = 2 * HYENA_ORDER * nrow * cb
        return 4 * (windows + scratch + spectra + tables)

    return next(c for c in (2 * V7X_LANES, V7X_LANES) if vmem_bytes(c) <= VMEM_LIMIT_BYTES)


def _hyena(u3, conv_w, conv_b, kf, hyena_bias, tabs, *, n1, n2, width):
    cb = _hyena_block_width(n1, n2)
    bsz, seq, _ = u3.shape
    ncb = width // cb
    nrow = kf.shape[1]
    pitch1 = 2 * n1 + ROW_PAD
    pitch2 = 2 * n2 + ROW_PAD
    nslab = cb // V7X_LANES
    seq_spec = lambda part: pl.BlockSpec((None, seq, cb), lambda c, b: (b, 0, part * ncb + c))
    tap_spec = lambda part: pl.BlockSpec((conv_w.shape[0], cb), lambda c, b: (0, part * ncb + c))
    bias_spec = lambda part: pl.BlockSpec((1, cb), lambda c, b: (0, part * ncb + c))
    consts = [tabs[k] for k in ('f1', 'g', 'm', 'mi', 'ms', 'mis')]
    return pl.pallas_call(
        functools.partial(_hyena_kernel, n1=n1, n2=n2),
        grid=(ncb, bsz),
        in_specs=[seq_spec(0), seq_spec(1), seq_spec(2),
                  tap_spec(0), tap_spec(1), tap_spec(2), bias_spec(0), bias_spec(1), bias_spec(2),
                  pl.BlockSpec((HYENA_ORDER, nrow, cb), lambda c, b: (0, 0, c)),
                  pl.BlockSpec((HYENA_ORDER, cb), lambda c, b: (0, c)),
                  *[_const_spec(t.shape) for t in consts]],
        out_specs=pl.BlockSpec((None, seq, cb), lambda c, b: (b, 0, c)),
        out_shape=jax.ShapeDtypeStruct((bsz, seq, width), F32),
        scratch_shapes=[pltpu.VMEM((seq, cb), F32), pltpu.VMEM((seq, cb), F32),
                        pltpu.VMEM((nslab, n2 * pitch1, V7X_LANES), F32),
                        pltpu.VMEM((nslab, n1 * pitch2, V7X_LANES), F32)],
        compiler_params=_compiler_params(("parallel", "parallel")),
        name="hyena",
    )(u3, u3, u3, conv_w, conv_w, conv_w, conv_b, conv_b, conv_b, kf, hyena_bias, *consts)


def _pool_tile(prev_ref, cur_ref, next_ref, *, n1, n2):
    j_per = cur_ref.shape[0] // n1
    assert POOL_HALO <= j_per
    seq = n1 * n2
    t = pl.program_id(1)
    first, last = t == 0, t == pl.num_programs(1) - 1
    rows = lax.broadcasted_iota(jnp.int32, (n1, V7X_LANES), 0)
    cols = []
    for g, win in enumerate(POOL_WINDOWS):
        lo = win // 2
        hi = win - 1 - lo
        lanes = slice(g * V7X_LANES, (g + 1) * V7X_LANES)
        memo = {}

        def blk(m):
            if m not in memo:
                if m < 0:
                    b = prev_ref[(j_per + m) * n1:(j_per + m + 1) * n1, lanes]
                    memo[m] = jnp.where(first, jnp.where(rows == 0, 0.0, pltpu.roll(b, 1, 0)), b)
                elif m >= j_per:
                    b = next_ref[(m - j_per) * n1:(m - j_per + 1) * n1, lanes]
                    memo[m] = jnp.where(last, jnp.where(rows == n1 - 1, 0.0, pltpu.roll(b, n1 - 1, 0)), b)
                else:
                    memo[m] = cur_ref[m * n1:(m + 1) * n1, lanes]
            return memo[m]

        outs = []
        for j in range(j_per):
            acc = blk(j - lo)
            for dlt in range(-lo + 1, hi + 1):
                acc = acc + blk(j + dlt)
            pos = rows * n2 + (t * j_per + j)
            cnt = jnp.minimum(pos + (win - lo), seq) - jnp.maximum(pos - lo, 0)
            outs.append(acc / cnt.astype(F32) - blk(j))
        cols.append(jnp.concatenate(outs, axis=0))
    return jnp.concatenate(cols, axis=1)


def _merge_ffn_kernel(x_ref, permt_ref, ap_ref, ac_ref, an_ref, z_ref, gate_ref, lng_ref, lnb_ref, wp_ref, bp_ref,
                      ps_ref, wpp_ref, whp_ref, wo_ref, bo_ref, g1_ref, b1_ref, w1_ref, bf1_ref, w2_ref, bf2_ref,
                      g2_ref, b2_ref, y_ref, *, n1, n2, ffn_chunk):
    tm, d = z_ref.shape
    ngrp = wp_ref.shape[0]
    gd = wp_ref.shape[1]
    pm = _pool_tile(ap_ref, ac_ref, an_ref, n1=n1, n2=n2).astype(BF16)
    p = jnp.concatenate([_dot(pm[:, g * gd:(g + 1) * gd], wp_ref[g]) for g in range(ngrp)], axis=1)
    p = (p + bp_ref[...]) * ps_ref[...]
    ya = _dot(p.astype(BF16), wpp_ref[...])
    yb = _dot(z_ref[...].astype(BF16), whp_ref[...])
    m = gate_ref[:, :d] * ya + gate_ref[:, d:] * yb
    m = _dot(permt_ref[...], m.astype(BF16)).astype(BF16)
    mo = _dot(m, wo_ref[...]) + bo_ref[...]
    h0 = _layer_norm(x_ref[...].reshape(tm, d), lng_ref[...], lnb_ref[...])
    h = _layer_norm(DN_ALPHA * h0 + mo, g1_ref[...], b1_ref[...])
    hb = h.astype(BF16)
    acc = jnp.zeros(h.shape, F32)
    for c in range(0, w1_ref.shape[1], ffn_chunk):
        u = jnp.maximum(_dot(hb, w1_ref[:, c:c + ffn_chunk]) + bf1_ref[:, c:c + ffn_chunk], 0.0)
        acc = acc + _dot((u * u).astype(BF16), w2_ref[c:c + ffn_chunk, :])
    y = _layer_norm(DN_ALPHA * h + acc + bf2_ref[...], g2_ref[...], b2_ref[...])
    y_ref[...] = y.reshape(y_ref.shape)


def _merge_ffn(xv, permt, a, z, gate, consts, *, n1, n2):
    bsz, seq, d = z.shape
    j_per = TOKEN_TILE // n1
    ntile = n2 // j_per
    tok = lambda w: pl.BlockSpec((None, TOKEN_TILE, w), lambda b, t: (b, t, 0))
    nat = pl.BlockSpec((None, n1, j_per, d), lambda b, t: (b, 0, t, 0))
    a_prev = pl.BlockSpec((None, TOKEN_TILE, a.shape[2]), lambda b, t: (b, (t + ntile - 1) % ntile, 0))
    a_next = pl.BlockSpec((None, TOKEN_TILE, a.shape[2]), lambda b, t: (b, (t + 1) % ntile, 0))
    return pl.pallas_call(
        functools.partial(_merge_ffn_kernel, n1=n1, n2=n2, ffn_chunk=1024),
        grid=(bsz, ntile),
        in_specs=[nat, _const_spec(permt.shape), a_prev, tok(a.shape[2]), a_next, tok(d), tok(gate.shape[2]),
                  *[_const_spec(c.shape) for c in consts]],
        out_specs=nat,
        out_shape=jax.ShapeDtypeStruct(xv.shape, F32),
        compiler_params=_compiler_params(("parallel", "parallel")),
        name="merge_ffn",
    )(xv, permt, a, a, a, z, gate, *consts)


def _encoder(x, p):
    bsz, seq, d = x.shape
    n1, n2 = _split_factors(seq)
    pool_w = p['w_pool_proj'].shape[0]
    hy_w = p['w_hyena_proj'].shape[0]
    row = lambda v: v.reshape(1, -1).astype(F32)
    tabs = {k: jnp.asarray(v, BF16) for k, v in _dft_tables(n1, n2).items()}

    xv = x.reshape(bsz, n1, n2, d)
    perm = _row_permutation(n1, TOKEN_TILE // n1)
    a, u3, gate = _inproj(xv, jnp.asarray(perm, BF16), row(p['ln_in_g']), row(p['ln_in_b']),
                          p['w_in'].astype(BF16), row(p['b_in']),
                          n1=n1, n2=n2, widths=(pool_w, HYENA_ORDER * hy_w + hy_w, 2 * d))
    kf = _filter_spectra(p, tabs, n1=n1, n2=n2, width=hy_w)
    z = _hyena(u3, p['conv_w'], row(p['conv_b']), kf, p['hyena_bias'], tabs, n1=n1, n2=n2, width=hy_w)
    merge_consts = [row(p['ln_in_g']), row(p['ln_in_b']), p['w_pool'].astype(BF16), row(p['b_pool']),
                    row(p['pool_scale']), p['w_pool_proj'].astype(BF16), p['w_hyena_proj'].astype(BF16),
                    p['w_o'].astype(BF16), row(p['b_o']), row(p['ln1_g']), row(p['ln1_b']),
                    p['w_ff1'].astype(BF16), row(p['b_ff1']), p['w_ff2'].astype(BF16), row(p['b_ff2']),
                    row(p['ln2_g']), row(p['ln2_b'])]
    y = _merge_ffn(xv, jnp.asarray(perm.T, BF16), a, z, gate, merge_consts, n1=n1, n2=n2)
    return y.reshape(bsz, seq, d)


def kernel(x_prompt, x_sample, ln_in_g, ln_in_b, w_in, b_in, w_pool, b_pool, pool_scale, w_pool_proj, conv_w, conv_b, w_f1, b_f1, freq_f1, w_f2, b_f2, freq_f2, w_f_out, decay_rate, hyena_bias, w_hyena_proj, w_o, b_o, ln1_g, ln1_b, w_ff1, b_ff1, w_ff2, b_ff2, ln2_g, ln2_b):
    layer = dict(w_in=w_in, b_in=b_in, w_pool=w_pool, b_pool=b_pool, pool_scale=pool_scale,
                 w_pool_proj=w_pool_proj, conv_w=conv_w, conv_b=conv_b, w_f1=w_f1, b_f1=b_f1, freq_f1=freq_f1,
                 w_f2=w_f2, b_f2=b_f2, freq_f2=freq_f2, w_f_out=w_f_out, decay_rate=decay_rate,
                 hyena_bias=hyena_bias, w_hyena_proj=w_hyena_proj, w_o=w_o, b_o=b_o, ln1_g=ln1_g, ln1_b=ln1_b,
                 w_ff1=w_ff1, b_ff1=b_ff1, w_ff2=w_ff2, b_ff2=b_ff2, ln2_g=ln2_g, ln2_b=ln2_b)
    assert all(v.shape[0] == DEPTH for v in layer.values())
    p = {k: v[0] for k, v in layer.items()}
    p.update(ln_in_g=ln_in_g, ln_in_b=ln_in_b)
    y_prompt = _encoder(x_prompt, p)
    y_sample = _encoder(x_sample, p)
    return (y_prompt, y_sample)
```

```python
import functools
import math

import numpy as np
import jax
import jax.numpy as jnp
from jax import lax
from jax.experimental import pallas as pl
from jax.experimental.pallas import tpu as pltpu

F32 = jnp.float32
BF16 = jnp.bfloat16

V7X_LANES = 128
V7X_SUBLANES = 8
V7X_VMEM_BYTES = 64 * 1024 * 1024
VMEM_LIMIT_BYTES = V7X_VMEM_BYTES - 8 * 1024 * 1024

POOL_WINDOWS = (2, 4, 8, 16)
POOL_HALO = max(max(w // 2, w - 1 - w // 2) for w in POOL_WINDOWS)
N_DIRS = 2
HYENA_ORDER = 2
POS_BANDS = 16
LN_EPS = 1e-5
L1_EPS = 1e-6
DEPTH = 1
DN_ALPHA = (2.0 * DEPTH) ** 0.25

ROW_PAD = V7X_SUBLANES
TOKEN_TILE = 512


def _compiler_params(semantics):
    return pltpu.CompilerParams(dimension_semantics=semantics, vmem_limit_bytes=VMEM_LIMIT_BYTES)


def _const_spec(shape):
    nd = len(shape)
    return pl.BlockSpec(shape, lambda *_: (0,) * nd, pipeline_mode=pl.Buffered(1))


def _split_factors(seq_len):
    n1 = 64
    assert seq_len % n1 == 0 and TOKEN_TILE % n1 == 0
    return n1, seq_len // n1


def _dft_tables(n1, n2):
    seq = n1 * n2
    i1 = np.arange(n1)
    th = np.pi * np.outer(i1, i1) / n1
    f1 = np.zeros((2 * n1, n1))
    f1[:n1] = np.cos(th)
    f1[n1] = (-1.0) ** i1
    f1[n1 + 1:] = -np.sin(th[1:])
    g = np.zeros((n1, 2 * n1))
    g[:, :n1] = np.cos(th.T)
    g[:, n1] = (-1.0) ** i1
    g[:, n1 + 1:] = -np.sin(th.T[:, 1:])
    i2 = np.arange(n2)
    base = 2 * np.pi * np.outer(i2, i2) / n2
    m = np.zeros((n1, 2 * n2, 2 * n2))
    mi = np.zeros((n1, 2 * n2, 2 * n2))
    for k in range(1, n1):
        phi = base + 2 * np.pi * k * i2[None, :] / (2 * seq)
        c, s = np.cos(phi), np.sin(phi)
        m[k] = np.block([[c, s], [-s, c]])
        mi[k] = np.block([[c.T, -s.T], [s.T, c.T]]) / seq
    phin = base + np.pi * i2[None, :] / n2
    z = np.zeros((n2, n2))
    ms = np.block([[np.cos(base), z], [-np.sin(base), z], [z, np.cos(phin)], [z, -np.sin(phin)]])
    mis = np.block([[np.cos(base).T, -np.sin(base).T, z, z],
                    [z, z, np.cos(phin).T, -np.sin(phin).T]]) / (2 * seq)
    return dict(f1=f1, g=g, m=m, mi=mi, ms=ms, mis=mis)


def _perm_positions(n1, n2):
    return (np.arange(n2)[:, None] + n2 * np.arange(n1)[None, :]).reshape(-1)


def _positional_features(seq_len, n1, n2):
    pos = _perm_positions(n1, n2).astype(np.float64)
    t = pos / (seq_len - 1)
    w = 2.0 * math.pi * pos / seq_len
    f = np.linspace(1e-4, POS_BANDS - 1, POS_BANDS)
    z = np.concatenate([t[:, None], np.cos(f[None] * w[:, None]), -np.sin(f[None] * w[:, None])], axis=-1)
    zp = np.zeros((seq_len, 40))
    zp[:, :z.shape[1]] = z
    return zp, t[:, None]


def _layer_norm(x, g, b):
    mu = jnp.mean(x, axis=-1, keepdims=True)
    xc = x - mu
    var = jnp.mean(xc * xc, axis=-1, keepdims=True)
    return xc * lax.rsqrt(var + LN_EPS) * g + b


def _dot(a, b):
    return jnp.dot(a, b, preferred_element_type=F32)


def _split_bf16(x):
    hi = x.astype(BF16)
    return hi, (x - hi.astype(F32)).astype(BF16)


def _dot3(a, b):
    a_hi, a_lo = _split_bf16(a)
    b_hi, b_lo = _split_bf16(b)
    return _dot(a_hi, b_hi) + (_dot(a_hi, b_lo) + _dot(a_lo, b_hi))


def _store_slabs(dst_ref, row0, val):
    rows = val.shape[0]
    for j in range(dst_ref.shape[0]):
        dst_ref[j, pl.ds(row0, rows), :] = val[:, j * V7X_LANES:(j + 1) * V7X_LANES]


def _gather_pair(src_ref, start_a, start_b, count, pitch):
    cols = []
    for j in range(src_ref.shape[0]):
        a = src_ref[j, pl.ds(start_a, count, stride=pitch), :]
        b = src_ref[j, pl.ds(start_b, count, stride=pitch), :]
        cols.append(jnp.concatenate([a, b], axis=0))
    return cols[0] if len(cols) == 1 else jnp.concatenate(cols, axis=1)


def _complex_mul(x, k, half):
    xr, xi = x[:half], x[half:]
    kr, ki = k[:half], k[half:]
    return jnp.concatenate([xr * kr - xi * ki, xr * ki + xi * kr], axis=0)


def _row_permutation(n1, j_per):
    i, j = np.meshgrid(np.arange(n1), np.arange(j_per), indexing="ij")
    p = np.zeros((n1 * j_per, n1 * j_per), np.float32)
    p[(j * n1 + i).ravel(), (i * j_per + j).ravel()] = 1.0
    return p


def _inproj_kernel(x_ref, perm_ref, g_ref, b_ref, w_ref, bias_ref, a_ref, u_ref, gate_ref, h_scr, *, chunk):
    n1, j_per, d = x_ref.shape
    for i in range(0, n1, V7X_SUBLANES):
        x = x_ref[i:i + V7X_SUBLANES].reshape(V7X_SUBLANES * j_per, d)
        h_scr[i * j_per:(i + V7X_SUBLANES) * j_per, :] = _layer_norm(x, g_ref[...], b_ref[...]).astype(BF16)
    h = _dot(perm_ref[...], h_scr[...]).astype(BF16)
    wa, wu = a_ref.shape[1], u_ref.shape[1]
    total = w_ref.shape[1]
    for c in range(0, total, chunk):
        y = _dot(h, w_ref[:, c:c + chunk]) + bias_ref[:, c:c + chunk]
        if c < wa:
            a_ref[:, c:c + chunk] = y
        elif c < wa + wu:
            u_ref[:, c - wa:c - wa + chunk] = y
        else:
            gate_ref[:, c - wa - wu:c - wa - wu + chunk] = jax.nn.sigmoid(y)


def _inproj(xv, perm, ln_g, ln_b, w_in, b_in, *, n1, n2, widths):
    bsz = xv.shape[0]
    d = w_in.shape[0]
    seq = n1 * n2
    j_per = TOKEN_TILE // n1
    wa, wu, wg = widths
    chunk = 512
    assert wa % chunk == 0 and wu % chunk == 0 and wg % chunk == 0 and n2 % j_per == 0
    out_shapes = tuple(jax.ShapeDtypeStruct((bsz, seq, w), F32) for w in widths)
    tok = lambda w: pl.BlockSpec((None, TOKEN_TILE, w), lambda b, t: (b, t, 0))
    return pl.pallas_call(
        functools.partial(_inproj_kernel, chunk=chunk),
        grid=(bsz, n2 // j_per),
        in_specs=[
            pl.BlockSpec((None, n1, j_per, d), lambda b, t: (b, 0, t, 0)),
            _const_spec(perm.shape), _const_spec((1, d)), _const_spec((1, d)),
            _const_spec(w_in.shape), _const_spec(b_in.shape),
        ],
        out_specs=[tok(wa), tok(wu), tok(wg)],
        out_shape=out_shapes,
        scratch_shapes=[pltpu.VMEM((TOKEN_TILE, d), BF16)],
        compiler_params=_compiler_params(("parallel", "parallel")),
        name="inproj",
    )(xv, perm, ln_g, ln_b, w_in, b_in)


def _filter_kernel(z_ref, t_ref, w1_ref, b1_ref, q1_ref, w2_ref, b2_ref, q2_ref, wo_ref, dec_ref,
                   f1_ref, m_ref, ms_ref, kf_ref, hid_ref, h_ref, s1_ref, *, n1, n2):
    seq = n1 * n2
    pitch = 2 * n1 + ROW_PAD
    hp = lax.Precision.HIGHEST
    rb = 256

    def mlp(i, c):
        r = pl.multiple_of(i * rb, rb)
        z = z_ref[pl.ds(r, rb), :]
        h = jnp.sin(q1_ref[...] * (jnp.dot(z, w1_ref[...], precision=hp, preferred_element_type=F32) + b1_ref[...]))
        h = jnp.sin(q2_ref[...] * (jnp.dot(h, w2_ref[...], precision=hp, preferred_element_type=F32) + b2_ref[...]))
        hid_ref[pl.ds(r, rb), :] = h
        return c

    @pl.when((pl.program_id(0) == 0) & (pl.program_id(1) == 0))
    def _():
        lax.fori_loop(0, seq // rb, mlp, 0, unroll=4)

    sums = []
    for dr in range(N_DIRS):
        rate = jnp.abs(dec_ref[dr:dr + 1, :])

        def gen(i, acc, dr=dr, rate=rate):
            r = pl.multiple_of(i * rb, rb)
            h = _dot3(hid_ref[pl.ds(r, rb), :], wo_ref[dr])
            h = h * jnp.exp(-t_ref[pl.ds(r, rb), :] * rate)
            if dr == 1:
                row = lax.broadcasted_iota(jnp.int32, h.shape, 0) + r
                h = jnp.where(row == 0, 0.0, h)
            h_ref[dr, pl.ds(r, rb), :] = h
            return acc + jnp.sum(jnp.abs(h), axis=0, keepdims=True)

        sums.append(lax.fori_loop(0, seq // rb, gen, jnp.zeros((1, h_ref.shape[2]), F32), unroll=8))

    denom = sums[0] + sums[1] + L1_EPS
    for dr in range(N_DIRS):
        def stage1(i, c, dr=dr):
            r = pl.multiple_of(i * n1, n1)
            blk = h_ref[dr, pl.ds(r, n1), :] / denom
            _store_slabs(s1_ref, pl.multiple_of(i * pitch, V7X_SUBLANES), _dot(f1_ref[...], blk.astype(BF16)))
            return c

        lax.fori_loop(0, n2, stage1, 0, unroll=16)

        def emit(row0, x, half, dr=dr):
            if dr == 0:
                kf_ref[pl.ds(row0, 2 * half), :] = x
            else:
                sign = jnp.where(lax.broadcasted_iota(jnp.int32, x.shape, 0) < half, 1.0, -1.0)
                kf_ref[pl.ds(row0, 2 * half), :] += sign * x

        b0 = _gather_pair(s1_ref, 0, n1, n2, pitch).astype(BF16)
        x0 = _dot(ms_ref[...], b0)
        emit(0, x0[:2 * n2], n2)
        emit(2 * n2, x0[2 * n2:], n2)

        def stage2(k, c, emit=emit):
            bk = _gather_pair(s1_ref, k, n1 + k, n2, pitch).astype(BF16)
            emit(pl.multiple_of((k + 1) * 2 * n2, 2 * n2), _dot(m_ref[k], bk), n2)
            return c

        lax.fori_loop(1, n1, stage2, 0, unroll=21)


def _filter_spectra(p, tabs, *, n1, n2, width):
    seq = n1 * n2
    cb = 2 * V7X_LANES
    nrow = 2 * n2 * (n1 + 1)
    pitch = 2 * n1 + ROW_PAD
    zfeat, tcol = _positional_features(seq, n1, n2)
    hid = p['w_f2'].shape[0]
    w1 = jnp.zeros((zfeat.shape[1], hid), F32).at[:p['w_f1'].shape[0]].set(p['w_f1'])
    wo = p['w_f_out'].reshape(hid, HYENA_ORDER, N_DIRS, width).transpose(1, 2, 0, 3)
    row = lambda v: v.reshape(1, -1)
    operands = [jnp.asarray(zfeat, F32), jnp.asarray(tcol, F32), w1, row(p['b_f1']), row(p['freq_f1']),
                p['w_f2'], row(p['b_f2']), row(p['freq_f2']), wo,
                p['decay_rate'].reshape(HYENA_ORDER, N_DIRS, width),
                tabs['f1'], tabs['m'], tabs['ms']]
    in_specs = [_const_spec(o.shape) for o in operands]
    in_specs[8] = pl.BlockSpec((None, N_DIRS, hid, cb), lambda c, o: (o, 0, 0, c))
    in_specs[9] = pl.BlockSpec((None, N_DIRS, cb), lambda c, o: (o, 0, c))
    return pl.pallas_call(
        functools.partial(_filter_kernel, n1=n1, n2=n2),
        grid=(width // cb, HYENA_ORDER),
        in_specs=in_specs,
        out_specs=pl.BlockSpec((None, nrow, cb), lambda c, o: (o, 0, c)),
        out_shape=jax.ShapeDtypeStruct((HYENA_ORDER, nrow, width), F32),
        scratch_shapes=[pltpu.VMEM((seq, hid), F32), pltpu.VMEM((N_DIRS, seq, cb), F32),
                        pltpu.VMEM((cb // V7X_LANES, n2 * pitch, V7X_LANES), F32)],
        compiler_params=_compiler_params(("arbitrary", "arbitrary")),
        name="filter_spectra",
    )(*operands)


def _hyena_kernel(x1_ref, x2_ref, v_ref, cw1_ref, cw2_ref, cwv_ref, cb1_ref, cb2_ref, cbv_ref,
                  kf_ref, hb_ref, f1_ref, g_ref, m_ref, mi_ref, ms_ref, mis_ref,
                  z_ref, u_scr, w_scr, s1_ref, s2_ref, *, n1, n2):
    pitch1 = 2 * n1 + ROW_PAD
    pitch2 = 2 * n2 + ROW_PAD
    cbw = z_ref.shape[1]
    rows = lax.broadcasted_iota(jnp.int32, (n1, cbw), 0)

    def short_conv(src, w_ref, b_ref, i):
        r = i * n1
        if i == 0:
            prev = jnp.where(rows == 0, 0.0, src[pl.ds((n2 - 1) * n1 - 1, n1), :])
        else:
            prev = src[pl.ds(r - n1, n1), :]
        if i == n2 - 1:
            nxt = jnp.where(rows == n1 - 1, 0.0, src[pl.ds(1, n1), :])
        else:
            nxt = src[pl.ds(r + n1, n1), :]
        return prev * w_ref[0:1, :] + src[pl.ds(r, n1), :] * w_ref[1:2, :] + nxt * w_ref[2:3, :] + b_ref[...]

    def stage1(i, u):
        _store_slabs(s1_ref, i * pitch1, _dot(f1_ref[...], u.astype(BF16)))

    def middle(o):
        b0 = _gather_pair(s1_ref, 0, n1, n2, pitch1).astype(BF16)
        x0 = _dot(ms_ref[...], b0)
        y0 = jnp.concatenate([_complex_mul(x0[:2 * n2], kf_ref[o, 0:2 * n2, :], n2),
                              _complex_mul(x0[2 * n2:], kf_ref[o, 2 * n2:4 * n2, :], n2)], axis=0)
        _store_slabs(s2_ref, 0, _dot(mis_ref[...], y0.astype(BF16)))
        for k in range(1, n1):
            bk = _gather_pair(s1_ref, k, n1 + k, n2, pitch1).astype(BF16)
            xk = _dot(m_ref[k], bk)
            yk = _complex_mul(xk, kf_ref[o, (k + 1) * 2 * n2:(k + 2) * 2 * n2, :], n2).astype(BF16)
            _store_slabs(s2_ref, k * pitch2, _dot(mi_ref[k], yk))

    def finish(o, gate_ref, gw_ref, gb_ref, u_in, z_out):
        for i in range(n2):
            dk = _gather_pair(s2_ref, i, n2 + i, n1, pitch2).astype(BF16)
            conv = _dot(g_ref[...], dk) + u_in[pl.ds(i * n1, n1), :] * hb_ref[o:o + 1, :]
            z_out[pl.ds(i * n1, n1), :] = short_conv(gate_ref, gw_ref, gb_ref, i) * conv

    for i in range(n2):
        u = short_conv(v_ref, cwv_ref, cbv_ref, i)
        u_scr[pl.ds(i * n1, n1), :] = u
        stage1(i, u)
    middle(0)
    finish(0, x1_ref, cw1_ref, cb1_ref, u_scr, w_scr)
    for i in range(n2):
        stage1(i, w_scr[pl.ds(i * n1, n1), :])
    middle(1)
    finish(1, x2_ref, cw2_ref, cb2_ref, w_scr, z_ref)


def _hyena_block_width(n1, n2):
    seq = n1 * n2
    nrow = 2 * n2 * (n1 + 1)

    def vmem_bytes(cb):
        windows = 2 * 4 * seq * cb
        scratch = 2 * seq * cb + n2 * (2 * n1 + ROW_PAD) * cb + n1 * (2 * n2 + ROW_PAD) * cb
        tables = 2 * (2 * n1 * (2 * n2) ** 2 + 2 * 4 * n2 * 2 * n2 + 2 * 2 * n1 * n1) // 4
        spectra = 2 * HYENA_ORDER * nrow * cb
        return 4 * (windows + scratch + spectra + tables)

    return next(c for c in (2 * V7X_LANES, V7X_LANES) if vmem_bytes(c) <= VMEM_LIMIT_BYTES)


def _hyena(u3, conv_w, conv_b, kf, hyena_bias, tabs, *, n1, n2, width):
    cb = _hyena_block_width(n1, n2)
    bsz, seq, _ = u3.shape
    ncb = width // cb
    nrow = kf.shape[1]
    pitch1 = 2 * n1 + ROW_PAD
    pitch2 = 2 * n2 + ROW_PAD
    nslab = cb // V7X_LANES
    seq_spec = lambda part: pl.BlockSpec((None, seq, cb), lambda c, b: (b, 0, part * ncb + c))
    tap_spec = lambda part: pl.BlockSpec((conv_w.shape[0], cb), lambda c, b: (0, part * ncb + c))
    bias_spec = lambda part: pl.BlockSpec((1, cb), lambda c, b: (0, part * ncb + c))
    consts = [tabs[k] for k in ('f1', 'g', 'm', 'mi', 'ms', 'mis')]
    return pl.pallas_call(
        functools.partial(_hyena_kernel, n1=n1, n2=n2),
        grid=(ncb, bsz),
        in_specs=[seq_spec(0), seq_spec(1), seq_spec(2),
                  tap_spec(0), tap_spec(1), tap_spec(2), bias_spec(0), bias_spec(1), bias_spec(2),
                  pl.BlockSpec((HYENA_ORDER, nrow, cb), lambda c, b: (0, 0, c)),
                  pl.BlockSpec((HYENA_ORDER, cb), lambda c, b: (0, c)),
                  *[_const_spec(t.shape) for t in consts]],
        out_specs=pl.BlockSpec((None, seq, cb), lambda c, b: (b, 0, c)),
        out_shape=jax.ShapeDtypeStruct((bsz, seq, width), F32),
        scratch_shapes=[pltpu.VMEM((seq, cb), F32), pltpu.VMEM((seq, cb), F32),
                        pltpu.VMEM((nslab, n2 * pitch1, V7X_LANES), F32),
                        pltpu.VMEM((nslab, n1 * pitch2, V7X_LANES), F32)],
        compiler_params=_compiler_params(("parallel", "parallel")),
        name="hyena",
    )(u3, u3, u3, conv_w, conv_w, conv_w, conv_b, conv_b, conv_b, kf, hyena_bias, *consts)


def _pool_tile(prev_ref, cur_ref, next_ref, *, n1, n2):
    j_per = cur_ref.shape[0] // n1
    assert POOL_HALO <= j_per
    seq = n1 * n2
    t = pl.program_id(1)
    first, last = t == 0, t == pl.num_programs(1) - 1
    rows = lax.broadcasted_iota(jnp.int32, (n1, V7X_LANES), 0)
    cols = []
    for g, win in enumerate(POOL_WINDOWS):
        lo = win // 2
        hi = win - 1 - lo
        lanes = slice(g * V7X_LANES, (g + 1) * V7X_LANES)
        memo = {}

        def blk(m):
            if m not in memo:
                if m < 0:
                    b = prev_ref[(j_per + m) * n1:(j_per + m + 1) * n1, lanes]
                    memo[m] = jnp.where(first, jnp.where(rows == 0, 0.0, pltpu.roll(b, 1, 0)), b)
                elif m >= j_per:
                    b = next_ref[(m - j_per) * n1:(m - j_per + 1) * n1, lanes]
                    memo[m] = jnp.where(last, jnp.where(rows == n1 - 1, 0.0, pltpu.roll(b, n1 - 1, 0)), b)
                else:
                    memo[m] = cur_ref[m * n1:(m + 1) * n1, lanes]
            return memo[m]

        outs = []
        for j in range(j_per):
            acc = blk(j - lo)
            for dlt in range(-lo + 1, hi + 1):
                acc = acc + blk(j + dlt)
            pos = rows * n2 + (t * j_per + j)
            cnt = jnp.minimum(pos + (win - lo), seq) - jnp.maximum(pos - lo, 0)
            outs.append(acc / cnt.astype(F32) - blk(j))
        cols.append(jnp.concatenate(outs, axis=0))
    return jnp.concatenate(cols, axis=1)


def _merge_ffn_kernel(x_ref, permt_ref, ap_ref, ac_ref, an_ref, z_ref, gate_ref, lng_ref, lnb_ref, wp_ref, bp_ref,
                      ps_ref, wpp_ref, whp_ref, wo_ref, bo_ref, g1_ref, b1_ref, w1_ref, bf1_ref, w2_ref, bf2_ref,
                      g2_ref, b2_ref, y_ref, *, n1, n2, ffn_chunk):
    tm, d = z_ref.shape
    ngrp = wp_ref.shape[0]
    gd = wp_ref.shape[1]
    pm = _pool_tile(ap_ref, ac_ref, an_ref, n1=n1, n2=n2).astype(BF16)
    p = jnp.concatenate([_dot(pm[:, g * gd:(g + 1) * gd], wp_ref[g]) for g in range(ngrp)], axis=1)
    p = (p + bp_ref[...]) * ps_ref[...]
    ya = _dot(p.astype(BF16), wpp_ref[...])
    yb = _dot(z_ref[...].astype(BF16), whp_ref[...])
    m = gate_ref[:, :d] * ya + gate_ref[:, d:] * yb
    m = _dot(permt_ref[...], m.astype(BF16)).astype(BF16)
    mo = _dot(m, wo_ref[...]) + bo_ref[...]
    h0 = _layer_norm(x_ref[...].reshape(tm, d), lng_ref[...], lnb_ref[...])
    h = _layer_norm(DN_ALPHA * h0 + mo, g1_ref[...], b1_ref[...])
    hb = h.astype(BF16)
    acc = jnp.zeros(h.shape, F32)
    for c in range(0, w1_ref.shape[1], ffn_chunk):
        u = jnp.maximum(_dot(hb, w1_ref[:, c:c + ffn_chunk]) + bf1_ref[:, c:c + ffn_chunk], 0.0)
        acc = acc + _dot((u * u).astype(BF16), w2_ref[c:c + ffn_chunk, :])
    y = _layer_norm(DN_ALPHA * h + acc + bf2_ref[...], g2_ref[...], b2_ref[...])
    y_ref[...] = y.reshape(y_ref.shape)


def _merge_ffn(xv, permt, a, z, gate, consts, *, n1, n2):
    bsz, seq, d = z.shape
    j_per = TOKEN_TILE // n1
    ntile = n2 // j_per
    tok = lambda w: pl.BlockSpec((None, TOKEN_TILE, w), lambda b, t: (b, t, 0))
    nat = pl.BlockSpec((None, n1, j_per, d), lambda b, t: (b, 0, t, 0))
    a_prev = pl.BlockSpec((None, TOKEN_TILE, a.shape[2]), lambda b, t: (b, (t + ntile - 1) % ntile, 0))
    a_next = pl.BlockSpec((None, TOKEN_TILE, a.shape[2]), lambda b, t: (b, (t + 1) % ntile, 0))
    return pl.pallas_call(
        functools.partial(_merge_ffn_kernel, n1=n1, n2=n2, ffn_chunk=1024),
        grid=(bsz, ntile),
        in_specs=[nat, _const_spec(permt.shape), a_prev, tok(a.shape[2]), a_next, tok(d), tok(gate.shape[2]),
                  *[_const_spec(c.shape) for c in consts]],
        out_specs=nat,
        out_shape=jax.ShapeDtypeStruct(xv.shape, F32),
        compiler_params=_compiler_params(("parallel", "parallel")),
        name="merge_ffn",
    )(xv, permt, a, a, a, z, gate, *consts)


def _encoder(x, p):
    bsz, seq, d = x.shape
    n1, n2 = _split_factors(seq)
    pool_w = p['w_pool_proj'].shape[0]
    hy_w = p['w_hyena_proj'].shape[0]
    row = lambda v: v.reshape(1, -1).astype(F32)
    tabs = {k: jnp.asarray(v, BF16) for k, v in _dft_tables(n1, n2).items()}

    xv = x.reshape(bsz, n1, n2, d)
    perm = _row_permutation(n1, TOKEN_TILE // n1)
    a, u3, gate = _inproj(xv, jnp.asarray(perm, BF16), row(p['ln_in_g']), row(p['ln_in_b']),
                          p['w_in'].astype(BF16), row(p['b_in']),
                          n1=n1, n2=n2, widths=(pool_w, HYENA_ORDER * hy_w + hy_w, 2 * d))
    kf = _filter_spectra(p, tabs, n1=n1, n2=n2, width=hy_w)
    z = _hyena(u3, p['conv_w'], row(p['conv_b']), kf, p['hyena_bias'], tabs, n1=n1, n2=n2, width=hy_w)
    merge_consts = [row(p['ln_in_g']), row(p['ln_in_b']), p['w_pool'].astype(BF16), row(p['b_pool']),
                    row(p['pool_scale']), p['w_pool_proj'].astype(BF16), p['w_hyena_proj'].astype(BF16),
                    p['w_o'].astype(BF16), row(p['b_o']), row(p['ln1_g']), row(p['ln1_b']),
                    p['w_ff1'].astype(BF16), row(p['b_ff1']), p['w_ff2'].astype(BF16), row(p['b_ff2']),
                    row(p['ln2_g']), row(p['ln2_b'])]
    y = _merge_ffn(xv, jnp.asarray(perm.T, BF16), a, z, gate, merge_consts, n1=n1, n2=n2)
    return y.reshape(bsz, seq, d)


def kernel(x_prompt, x_sample, ln_in_g, ln_in_b, w_in, b_in, w_pool, b_pool, pool_scale, w_pool_proj, conv_w, conv_b, w_f1, b_f1, freq_f1, w_f2, b_f2, freq_f2, w_f_out, decay_rate, hyena_bias, w_hyena_proj, w_o, b_o, ln1_g, ln1_b, w_ff1, b_ff1, w_ff2, b_ff2, ln2_g, ln2_b):
    layer = dict(w_in=w_in, b_in=b_in, w_pool=w_pool, b_pool=b_pool, pool_scale=pool_scale,
                 w_pool_proj=w_pool_proj, conv_w=conv_w, conv_b=conv_b, w_f1=w_f1, b_f1=b_f1, freq_f1=freq_f1,
                 w_f2=w_f2, b_f2=b_f2, freq_f2=freq_f2, w_f_out=w_f_out, decay_rate=decay_rate,
                 hyena_bias=hyena_bias, w_hyena_proj=w_hyena_proj, w_o=w_o, b_o=b_o, ln1_g=ln1_g, ln1_b=ln1_b,
                 w_ff1=w_ff1, b_ff1=b_ff1, w_ff2=w_ff2, b_ff2=b_ff2, ln2_g=ln2_g, ln2_b=ln2_b)
    assert all(v.shape[0] == DEPTH for v in layer.values())
    p = {k: v[0] for k, v in layer.items()}
    p.update(ln_in_g=ln_in_g, ln_in_b=ln_in_b)
    y_prompt = _encoder(x_prompt, p)
    y_sample = _encoder(x_sample, p)
    return (y_prompt, y_sample)
```
